```python
import jax, jax.numpy as jnp
from jax import lax
import numpy as np

D_MODEL = 1024
BATCH = 4
SEQ = 8192
DEPTH = 2

N_META = 16
CONV_GROUPS = 8
CONV_GROUP_DIM = 64
D_CONV = CONV_GROUPS * CONV_GROUP_DIM
CONV_WIDTH = 3
MLA_HEADS = 8
QK_NOPE = 64
QK_ROPE = 32
V_HEAD = 64
Q_LORA = 256
KV_LORA = 128
D_MLA = MLA_HEADS * V_HEAD
ROPE_BASE = 10000.0
Q_BLOCK = 128
NEG_INF = -1e30
D_FF = 2816
N_BRANCH = 2
ALPHA = (2 * DEPTH) ** 0.25
BETA = (8 * DEPTH) ** -0.25
LN_EPS = 1e-5
RMS_EPS = 1e-6
IN_SPLITS = (D_CONV, D_CONV, D_CONV, Q_LORA, KV_LORA, QK_ROPE, D_MODEL, D_MODEL)
D_IN = sum(IN_SPLITS)

kernel_name = 'hybrid_shortconv_mla_macaron_deepnorm'


def layer_norm(x, g, b):
    xf = x.astype(jnp.float32)
    mu = jnp.mean(xf, axis=-1, keepdims=True)
    var = jnp.mean(jnp.square(xf - mu), axis=-1, keepdims=True)
    return ((xf - mu) * lax.rsqrt(var + LN_EPS) * g + b).astype(x.dtype)


def rms_norm(x, g):
    xf = x.astype(jnp.float32)
    return (xf * lax.rsqrt(jnp.mean(jnp.square(xf), axis=-1, keepdims=True) + RMS_EPS) * g).astype(x.dtype)


def rope_tables(T):
    inv_freq = 1.0 / (ROPE_BASE ** (jnp.arange(0, QK_ROPE, 2, dtype=jnp.float32) / QK_ROPE))
    ang = jnp.arange(T, dtype=jnp.float32)[:, None] * inv_freq[None, :]
    return jnp.cos(ang), jnp.sin(ang)


def apply_rope(x, cos, sin):
    x1, x2 = jnp.split(x.astype(jnp.float32), 2, axis=-1)
    return jnp.concatenate([x1 * cos - x2 * sin, x2 * cos + x1 * sin], axis=-1).astype(x.dtype)


def swiglu(x, w_up, w_down):
    gate, up = jnp.split(x @ w_up, 2, axis=-1)
    return (jax.nn.silu(gate) * up) @ w_down


def causal_short_conv(u, w):
    T = u.shape[1]
    up = jnp.pad(u, ((0, 0), (CONV_WIDTH - 1, 0), (0, 0)))
    out = up[:, 0:T] * w[0]
    for k in range(1, CONV_WIDTH):
        out = out + up[:, k:k + T] * w[k]
    return out


def mla_causal_attention(q_nope, q_rope, k_nope, k_rope, v):
    bsz, T = q_nope.shape[:2]
    n_blocks = -(-T // Q_BLOCK)
    pad = n_blocks * Q_BLOCK - T
    scale = (QK_NOPE + QK_ROPE) ** -0.5

    def to_blocks(a):
        a = jnp.pad(a, ((0, 0), (0, pad), (0, 0), (0, 0)))
        return jnp.moveaxis(a.reshape(bsz, n_blocks, Q_BLOCK, *a.shape[2:]), 1, 0)

    q_pos = jnp.arange(n_blocks * Q_BLOCK, dtype=jnp.int32).reshape(n_blocks, Q_BLOCK)
    k_pos = jnp.arange(T, dtype=jnp.int32)

    def one_block(args):
        qn, qr, qp = args
        s = (jnp.einsum('bqhd,bkhd->bhqk', qn, k_nope)
             + jnp.einsum('bqhr,bkr->bhqk', qr, k_rope)).astype(jnp.float32) * scale
        s = jnp.where(k_pos[None, :] <= qp[:, None], s, NEG_INF)
        p = jax.nn.softmax(s, axis=-1).astype(v.dtype)
        return jnp.einsum('bhqk,bkhd->bqhd', p, v)

    out = lax.map(one_block, (to_blocks(q_nope), to_blocks(q_rope), q_pos))
    out = jnp.moveaxis(out, 0, 1).reshape(bsz, n_blocks * Q_BLOCK, MLA_HEADS, V_HEAD)
    return out[:, :T]


def hybrid_mixer(x, w_in, b_gate, conv_w, q_norm_g, w_uq, kv_norm_g, w_ukv, w_br_conv, w_br_mla, w_o, cos, sin):
    bsz, T, _ = x.shape
    cuts = [int(c) for c in np.cumsum(IN_SPLITS)[:-1]]
    b_in, c_in, h_in, c_q, c_kv, k_r, g_conv, g_mla = jnp.split(x @ w_in, cuts, axis=-1)
    y_conv = b_in * causal_short_conv(c_in * h_in, conv_w)
    q = (rms_norm(c_q, q_norm_g) @ w_uq).reshape(bsz, T, MLA_HEADS, QK_NOPE + QK_ROPE)
    q_nope = q[..., :QK_NOPE]
    q_rope = apply_rope(q[..., QK_NOPE:], cos[None, :, None], sin[None, :, None])
    kv = (rms_norm(c_kv, kv_norm_g) @ w_ukv).reshape(bsz, T, MLA_HEADS, QK_NOPE + V_HEAD)
    k_nope, v = kv[..., :QK_NOPE], kv[..., QK_NOPE:]
    k_rope = apply_rope(k_r, cos[None], sin[None])
    y_mla = mla_causal_attention(q_nope, q_rope, k_nope, k_rope, v).reshape(bsz, T, D_MLA)
    merged = (jax.nn.sigmoid(g_conv + b_gate[0]) * (y_conv @ w_br_conv)
              + jax.nn.sigmoid(g_mla + b_gate[1]) * (y_mla @ w_br_mla))
    return merged @ w_o


def setup_inputs(seed: int = 0) -> dict:
    key = jax.random.key(seed)
    ks = jax.random.split(key, 18)
    L = DEPTH

    def dense(k, shape, scale=1.0):
        return jax.random.normal(k, shape, jnp.float32) * (scale * shape[-2] ** -0.5)

    def near_one(k, shape):
        return 1.0 + 0.02 * jax.random.normal(k, shape, jnp.float32)

    def small(k, shape):
        return 0.02 * jax.random.normal(k, shape, jnp.float32)

    return {
        'x': jax.random.normal(ks[0], (BATCH, SEQ, D_MODEL), jnp.float32),
        'meta_tokens': jax.random.normal(ks[1], (N_META, D_MODEL), jnp.float32),
        'ffn1_w_up': dense(ks[2], (L, D_MODEL, 2 * D_FF)),
        'ffn1_w_down': dense(ks[3], (L, D_FF, D_MODEL), BETA),
        'mix_w_in': dense(ks[4], (L, D_MODEL, D_IN)),
        'mix_b_gate': small(ks[5], (L, N_BRANCH, D_MODEL)),
        'conv_w': dense(ks[6], (L, CONV_WIDTH, D_CONV)),
        'q_norm_g': near_one(ks[7], (L, Q_LORA)),
        'w_uq': dense(ks[8], (L, Q_LORA, MLA_HEADS * (QK_NOPE + QK_ROPE))),
        'kv_norm_g': near_one(ks[9], (L, KV_LORA)),
        'w_ukv': dense(ks[10], (L, KV_LORA, MLA_HEADS * (QK_NOPE + V_HEAD))),
        'w_br_conv': dense(ks[11], (L, D_CONV, D_MODEL)),
        'w_br_mla': dense(ks[12], (L, D_MLA, D_MODEL)),
        'w_o': dense(ks[13], (L, D_MODEL, D_MODEL), BETA),
        'ffn2_w_up': dense(ks[14], (L, D_MODEL, 2 * D_FF)),
        'ffn2_w_down': dense(ks[15], (L, D_FF, D_MODEL), BETA),
        'ln_g': near_one(ks[16], (L, 3, D_MODEL)),
        'ln_b': small(ks[17], (L, 3, D_MODEL)),
    }


def reference(x, meta_tokens, ffn1_w_up, ffn1_w_down, mix_w_in, mix_b_gate, conv_w, q_norm_g, w_uq,
              kv_norm_g, w_ukv, w_br_conv, w_br_mla, w_o, ffn2_w_up, ffn2_w_down, ln_g, ln_b):
    bsz = x.shape[0]
    meta = jnp.broadcast_to(meta_tokens[None].astype(x.dtype), (bsz, N_META, D_MODEL))
    h = jnp.concatenate([meta, x], axis=1)
    cos, sin = rope_tables(h.shape[1])
    for l in range(DEPTH):
        h = layer_norm(ALPHA * h + 0.5 * swiglu(h, ffn1_w_up[l], ffn1_w_down[l]), ln_g[l, 0], ln_b[l, 0])
        mix = hybrid_mixer(h, mix_w_in[l], mix_b_gate[l], conv_w[l], q_norm_g[l], w_uq[l], kv_norm_g[l],
                           w_ukv[l], w_br_conv[l], w_br_mla[l], w_o[l], cos, sin)
        h = layer_norm(ALPHA * h + mix, ln_g[l, 1], ln_b[l, 1])
        h = layer_norm(ALPHA * h + 0.5 * swiglu(h, ffn2_w_up[l], ffn2_w_down[l]), ln_g[l, 2], ln_b[l, 2])
    return h[:, N_META:]
```

```python
import functools
import math

import jax
import jax.numpy as jnp
from jax import lax
from jax.experimental import pallas as pl
from jax.experimental.pallas import tpu as pltpu

F32 = jnp.float32
BF16 = jnp.bfloat16

N_META = 16
CONV_WIDTH = 3
D_CONV = 512
MLA_HEADS = 8
QK_NOPE = 64
QK_ROPE = 32
V_HEAD = 64
Q_LORA = 256
KV_LORA = 128
ROPE_BASE = 10000.0
LN_EPS = 1e-5
RMS_EPS = 1e-6
NEG_INF = -1e30

LANES = 128
SUBLANES = 8
MXU_DIM = 256
VMEM_LIMIT_BYTES = 56 * 1024 * 1024

HEADS_PER_STEP = 2
ROW_TILE = 512
ATTN_TILE = 512

_C_B, _C_C, _C_H = 0, 512, 1024
_C_Q = 1536
_C_KV = 1792
_C_KR = 1920
_C_GC = 2048
_C_GM = 3072
D_IN_EXT = 4096


def _layer_norm(z, g, b):
    mu = jnp.mean(z, axis=-1, keepdims=True)
    d = z - mu
    var = jnp.mean(d * d, axis=-1, keepdims=True)
    return d * lax.rsqrt(var + LN_EPS) * g + b


def _rms_norm(z, g):
    return z * lax.rsqrt(jnp.mean(z * z, axis=-1, keepdims=True) + RMS_EPS) * g


def _dot(a, b):
    return jnp.dot(a, b, preferred_element_type=F32)


def _dot_nt(a, b):
    return lax.dot_general(a, b, (((1,), (1,)), ((), ())), preferred_element_type=F32)


def _resident(shape):
    nd = len(shape)
    return pl.BlockSpec(shape, lambda *_: (0,) * nd, pipeline_mode=pl.Buffered(1))


def _params(n_axes):
    return pltpu.CompilerParams(dimension_semantics=("arbitrary",) * n_axes,
                                vmem_limit_bytes=VMEM_LIMIT_BYTES)


def _ffn_ln_kernel(x_ref, wup_ref, wdown_ref, g_ref, b_ref, o_ref, act_ref, *, alpha):
    n_chunks, _, two_cw = wup_ref.shape
    cw = two_cw // 2
    x = x_ref[...]
    xb = x.astype(BF16)
    for c in range(n_chunks):
        h = _dot(xb, wup_ref[c])
        gate = h[:, :cw]
        up = h[:, cw:]
        act_ref[:, c * cw:(c + 1) * cw] = (gate * jax.nn.sigmoid(gate) * up).astype(BF16)
    y = _dot(act_ref[...], wdown_ref[...])
    o_ref[...] = _layer_norm(alpha * x + 0.5 * y, g_ref[...], b_ref[...])


def _ffn_ln(x, wup, wdown, g, b, *, alpha, tm):
    n, d = x.shape
    d_ff = wdown.shape[0]
    row = pl.BlockSpec((tm, d), lambda i: (i, 0))
    return pl.pallas_call(
        functools.partial(_ffn_ln_kernel, alpha=alpha),
        grid=(n // tm,),
        in_specs=[row, _resident(wup.shape), _resident(wdown.shape),
                  _resident(g.shape), _resident(b.shape)],
        out_specs=row,
        out_shape=jax.ShapeDtypeStruct((n, d), F32),
        scratch_shapes=[pltpu.VMEM((tm, d_ff), BF16)],
        compiler_params=_params(1),
        name="ffn_ln",
    )(x, wup, wdown, g, b)


def _mixer_in_kernel(h_ref, tab_ref, halo_ref, win_ref, convw_ref, bg_ref, qg_ref, wq_ref,
                     kvg_ref, wk_ref, wv_ref, wbrc_ref,
                     mconv_ref, gmla_ref, q_ref, k_ref, v_ref, utail_ref, ubuf_ref,
                     *, tiles_per_seq):
    tm = h_ref.shape[0]
    i = pl.program_id(0)
    hb = h_ref[...].astype(BF16)

    zc = _dot(hb, win_ref[:, _C_C:_C_C + 2 * D_CONV])
    u = zc[:, :D_CONV] * zc[:, D_CONV:]

    @pl.when(i % tiles_per_seq == 0)
    def _():
        ubuf_ref[0:SUBLANES, :] = halo_ref[...]

    @pl.when(i % tiles_per_seq != 0)
    def _():
        ubuf_ref[0:SUBLANES, :] = ubuf_ref[tm:tm + SUBLANES, :]

    ubuf_ref[SUBLANES:SUBLANES + tm, :] = u
    utail_ref[...] = u[tm - SUBLANES:, :]
    conv = convw_ref[CONV_WIDTH - 1:CONV_WIDTH, :] * u
    for k in range(CONV_WIDTH - 1):
        shift = CONV_WIDTH - 1 - k
        conv = conv + convw_ref[k:k + 1, :] * ubuf_ref[SUBLANES - shift:SUBLANES - shift + tm, :]
    zb = _dot(hb, win_ref[:, _C_B:_C_B + D_CONV])
    yc = _dot((zb * conv).astype(BF16), wbrc_ref[...])
    gc = _dot(hb, win_ref[:, _C_GC:_C_GC + yc.shape[1]])
    mconv_ref[...] = (jax.nn.sigmoid(gc + bg_ref[0:1, :]) * yc).astype(BF16)

    gm = _dot(hb, win_ref[:, _C_GM:_C_GM + yc.shape[1]])
    gmla_ref[...] = jax.nn.sigmoid(gm + bg_ref[1:2, :]).astype(BF16)

    cq = _rms_norm(_dot(hb, win_ref[:, _C_Q:_C_Q + Q_LORA]), qg_ref[...]).astype(BF16)
    q2 = _dot(cq, wq_ref[...])
    half = q2.shape[1] // 2
    tq_cos = tab_ref[:, 0:LANES]
    tq_sin = tab_ref[:, LANES:2 * LANES]
    for hd in range(MLA_HEADS):
        lo = hd * LANES
        q_ref[:, lo:lo + LANES] = (q2[:, lo:lo + LANES] * tq_cos
                                   + q2[:, half + lo:half + lo + LANES] * tq_sin).astype(BF16)

    zkv = _dot(hb, win_ref[:, _C_KV:_C_KV + 2 * LANES])
    ckv = _rms_norm(zkv[:, :KV_LORA], kvg_ref[...]).astype(BF16)
    krb = zkv[:, KV_LORA:]
    kr = (krb * tab_ref[:, 2 * LANES:3 * LANES]
          + pltpu.roll(krb, LANES - QK_ROPE, axis=1) * tab_ref[:, 3 * LANES:4 * LANES])
    kcat = jnp.concatenate([ckv, kr.astype(BF16)], axis=1)
    k_ref[...] = _dot(kcat, wk_ref[...]).astype(BF16)
    v_ref[...] = _dot(ckv, wv_ref[...]).astype(BF16)


def _mixer_in(h, tab, halo, w, *, tm, tiles_per_seq):
    n, d = h.shape
    row = lambda width: pl.BlockSpec((tm, width), lambda i: (i, 0))
    tab_spec = pl.BlockSpec((tm, tab.shape[1]), lambda i: (i % tiles_per_seq, 0))
    weights = [w["win"], w["convw"], w["bg"], w["qg"], w["wq"], w["kvg"], w["wk"], w["wv"], w["wbrc"]]
    d_k = MLA_HEADS * LANES
    d_v = MLA_HEADS * V_HEAD
    return pl.pallas_call(
        functools.partial(_mixer_in_kernel, tiles_per_seq=tiles_per_seq),
        grid=(n // tm,),
        in_specs=[row(d), tab_spec, _resident(halo.shape)] + [_resident(a.shape) for a in weights],
        out_specs=[row(d), row(d), row(d_k), row(d_k), row(d_v),
                   pl.BlockSpec((SUBLANES, D_CONV), lambda i: (0, 0))],
        out_shape=[jax.ShapeDtypeStruct((n, d), BF16), jax.ShapeDtypeStruct((n, d), BF16),
                   jax.ShapeDtypeStruct((n, d_k), BF16), jax.ShapeDtypeStruct((n, d_k), BF16),
                   jax.ShapeDtypeStruct((n, d_v), BF16),
                   jax.ShapeDtypeStruct((SUBLANES, D_CONV), F32)],
        scratch_shapes=[pltpu.VMEM((tm + SUBLANES, D_CONV), F32)],
        compiler_params=_params(1),
        name="mixer_in",
    )(h, tab, halo, *weights)


def _softmax_block(qh, kj, vj, m, l, acc, mask=None):
    s = _dot_nt(qh, kj)
    if mask is not None:
        s = jnp.where(mask, s, NEG_INF)
    m_new = jnp.maximum(m, jnp.max(s, axis=1, keepdims=True))
    a = jnp.exp2(m - m_new)
    p = jnp.exp2(s - m_new)
    l = a * l + jnp.sum(p, axis=1, keepdims=True)
    acc = a * acc + _dot(p.astype(BF16), vj)
    return m_new, l, acc


def _attn_kernel(q_ref, k_ref, v_ref, km_ref, vm_ref, o_ref):
    tq = q_ref.shape[0]
    i = pl.program_id(2)
    row = lax.broadcasted_iota(jnp.int32, (tq, tq), 0)
    col = lax.broadcasted_iota(jnp.int32, (tq, tq), 1)
    outs = []
    for hd in range(HEADS_PER_STEP):
        lo = hd * LANES
        qh = q_ref[:, lo:lo + LANES]
        s0 = _dot_nt(qh, km_ref[:, lo:lo + LANES])
        m = jnp.max(s0, axis=1, keepdims=True)
        p0 = jnp.exp2(s0 - m)
        l = jnp.sum(p0, axis=1, keepdims=True)
        acc = _dot(p0.astype(BF16), vm_ref[...])

        def body(j, carry, qh=qh, lo=lo):
            start = pl.multiple_of(j * tq, tq)
            return _softmax_block(qh, k_ref[pl.ds(start, tq), lo:lo + LANES],
                                  v_ref[pl.ds(start, tq), :], *carry)

        m, l, acc = lax.fori_loop(0, i, body, (m, l, acc))
        start = pl.multiple_of(i * tq, tq)
        m, l, acc = _softmax_block(qh, k_ref[pl.ds(start, tq), lo:lo + LANES],
                                   v_ref[pl.ds(start, tq), :], m, l, acc, mask=col <= row)
        outs.append(acc / l)
    lane = lax.broadcasted_iota(jnp.int32, outs[0].shape, 1)
    o_ref[...] = jnp.where(lane < V_HEAD, outs[0], outs[1]).astype(BF16)


def _attention(q, k, v, k_meta, v_meta, *, batch, seq, tq):
    nq = seq // tq
    qw = HEADS_PER_STEP * LANES
    vw = HEADS_PER_STEP * V_HEAD
    return pl.pallas_call(
        _attn_kernel,
        grid=(batch, MLA_HEADS // HEADS_PER_STEP, nq),
        in_specs=[pl.BlockSpec((tq, qw), lambda b, hp, i: (b * nq + i, hp)),
                  pl.BlockSpec((seq, qw), lambda b, hp, i: (b, hp)),
                  pl.BlockSpec((seq, vw), lambda b, hp, i: (b, hp)),
                  pl.BlockSpec((N_META, qw), lambda b, hp, i: (0, hp)),
                  pl.BlockSpec((N_META, vw), lambda b, hp, i: (0, hp))],
        out_specs=pl.BlockSpec((tq, vw), lambda b, hp, i: (b * nq + i, hp)),
        out_shape=jax.ShapeDtypeStruct((batch * seq, MLA_HEADS * V_HEAD), BF16),
        compiler_params=_params(3),
        name="attention",
    )(q, k, v, k_meta, v_meta)


def _attn_meta_kernel(q_ref, k_ref, v_ref, o_ref):
    t = q_ref.shape[0]
    row = lax.broadcasted_iota(jnp.int32, (t, t), 0)
    col = lax.broadcasted_iota(jnp.int32, (t, t), 1)
    outs = []
    for hd in range(HEADS_PER_STEP):
        lo = hd * LANES
        s = jnp.where(col <= row, _dot_nt(q_ref[:, lo:lo + LANES], k_ref[:, lo:lo + LANES]), NEG_INF)
        p = jnp.exp2(s - jnp.max(s, axis=1, keepdims=True))
        l = jnp.sum(p, axis=1, keepdims=True)
        outs.append(_dot(p.astype(BF16), v_ref[...]) / l)
    lane = lax.broadcasted_iota(jnp.int32, outs[0].shape, 1)
    o_ref[...] = jnp.where(lane < V_HEAD, outs[0], outs[1]).astype(BF16)


def _attention_meta(q, k, v):
    t = q.shape[0]
    qw = HEADS_PER_STEP * LANES
    vw = HEADS_PER_STEP * V_HEAD
    col = lambda width: pl.BlockSpec((t, width), lambda hp: (0, hp))
    return pl.pallas_call(
        _attn_meta_kernel,
        grid=(MLA_HEADS // HEADS_PER_STEP,),
        in_specs=[col(qw), col(qw), col(vw)],
        out_specs=col(vw),
        out_shape=jax.ShapeDtypeStruct((t, MLA_HEADS * V_HEAD), BF16),
        compiler_params=_params(1),
        name="attention_meta",
    )(q, k, v)


def _mixer_out_kernel(h_ref, o_ref, mconv_ref, gmla_ref, wbrm_ref, wo_ref, g_ref, b_ref, out_ref,
                      *, alpha):
    ym = _dot(o_ref[...], wbrm_ref[...])
    merged = mconv_ref[...].astype(F32) + gmla_ref[...].astype(F32) * ym
    mix = _dot(merged.astype(BF16), wo_ref[...])
    out_ref[...] = _layer_norm(alpha * h_ref[...] + mix, g_ref[...], b_ref[...])


def _mixer_out(h, o, mconv, gmla, wbrm, wo, g, b, *, alpha, tm):
    n, d = h.shape
    row = lambda width: pl.BlockSpec((tm, width), lambda i: (i, 0))
    return pl.pallas_call(
        functools.partial(_mixer_out_kernel, alpha=alpha),
        grid=(n // tm,),
        in_specs=[row(d), row(o.shape[1]), row(d), row(d), _resident(wbrm.shape),
                  _resident(wo.shape), _resident(g.shape), _resident(b.shape)],
        out_specs=row(d),
        out_shape=jax.ShapeDtypeStruct((n, d), F32),
        compiler_params=_params(1),
        name="mixer_out",
    )(h, o, mconv, gmla, wbrm, wo, g, b)


def _pack_ffn_up(w_up):
    d, two_ff = w_up.shape
    n_chunks = two_ff // 2 // MXU_DIM
    w = w_up.reshape(d, 2, n_chunks, MXU_DIM).transpose(2, 0, 1, 3)
    return w.reshape(n_chunks, d, 2 * MXU_DIM).astype(BF16)


def _pack_mixer(w_in, b_gate, conv_w, q_norm_g, w_uq, kv_norm_g, w_ukv, w_br_conv):
    d = w_in.shape[0]
    half_rope = QK_ROPE // 2
    c_kr = 3 * D_CONV + Q_LORA + KV_LORA
    kr = w_in[:, c_kr:c_kr + QK_ROPE]
    kr_swapped = jnp.concatenate([kr[:, half_rope:], kr[:, :half_rope]], axis=1)
    kr_block = jnp.concatenate([kr, kr_swapped, jnp.zeros((d, LANES - 2 * QK_ROPE), F32)], axis=1)
    win = jnp.concatenate([w_in[:, :c_kr], kr_block, w_in[:, c_kr + QK_ROPE:]], axis=1)
    assert win.shape[1] == D_IN_EXT

    wq = w_uq.reshape(Q_LORA, MLA_HEADS, QK_NOPE + QK_ROPE)
    nope, rope = wq[:, :, :QK_NOPE], wq[:, :, QK_NOPE:]
    rope_swapped = jnp.concatenate([rope[:, :, half_rope:], rope[:, :, :half_rope]], axis=2)
    pad = jnp.zeros((Q_LORA, MLA_HEADS, LANES - QK_NOPE - QK_ROPE), F32)
    qa = jnp.concatenate([nope, rope, pad], axis=2).reshape(Q_LORA, MLA_HEADS * LANES)
    qb = jnp.concatenate([jnp.zeros_like(nope), rope_swapped, pad], axis=2).reshape(Q_LORA, MLA_HEADS * LANES)

    wkv = w_ukv.reshape(KV_LORA, MLA_HEADS, QK_NOPE + V_HEAD)
    k_nope = jnp.concatenate([wkv[:, :, :QK_NOPE],
                              jnp.zeros((KV_LORA, MLA_HEADS, LANES - QK_NOPE), F32)], axis=2)
    place = jnp.concatenate([jnp.zeros((QK_ROPE, QK_NOPE), F32), jnp.eye(QK_ROPE, dtype=F32),
                             jnp.zeros((QK_ROPE, LANES - QK_NOPE - QK_ROPE), F32)], axis=1)
    wk = jnp.concatenate([k_nope.reshape(KV_LORA, MLA_HEADS * LANES),
                          jnp.tile(place, (1, MLA_HEADS)),
                          jnp.zeros((LANES - QK_ROPE, MLA_HEADS * LANES), F32)], axis=0)
    wv = wkv[:, :, QK_NOPE:].reshape(KV_LORA, MLA_HEADS * V_HEAD)
    return dict(win=win.astype(BF16), convw=conv_w, bg=b_gate, qg=q_norm_g[None, :],
                wq=jnp.concatenate([qa, qb], axis=1).astype(BF16), kvg=kv_norm_g[None, :],
                wk=wk.astype(BF16), wv=wv.astype(BF16), wbrc=w_br_conv.astype(BF16))


def _rope_table(positions):
    inv_freq = 1.0 / (ROPE_BASE ** (jnp.arange(0, QK_ROPE, 2, dtype=F32) / QK_ROPE))
    ang = positions.astype(F32)[:, None] * inv_freq[None, :]
    cos, sin = jnp.cos(ang), jnp.sin(ang)
    t = positions.shape[0]
    c = (QK_NOPE + QK_ROPE) ** -0.5 * math.log2(math.e)
    zeros = lambda w: jnp.zeros((t, w), F32)
    pad = LANES - QK_NOPE - QK_ROPE
    q_cos = jnp.concatenate([jnp.full((t, QK_NOPE), c, F32), c * cos, c * cos, zeros(pad)], axis=1)
    q_sin = jnp.concatenate([zeros(QK_NOPE), -c * sin, c * sin, zeros(pad)], axis=1)
    k_cos = jnp.concatenate([cos, cos, zeros(LANES - QK_ROPE)], axis=1)
    k_sin = jnp.concatenate([-sin, sin, zeros(LANES - QK_ROPE)], axis=1)
    return jnp.concatenate([q_cos, q_sin, k_cos, k_sin], axis=1)


def kernel(x, meta_tokens, ffn1_w_up, ffn1_w_down, mix_w_in, mix_b_gate, conv_w, q_norm_g, w_uq,
           kv_norm_g, w_ukv, w_br_conv, w_br_mla, w_o, ffn2_w_up, ffn2_w_down, ln_g, ln_b):
    batch, seq, d = x.shape
    depth = ffn1_w_up.shape[0]
    alpha = (2 * depth) ** 0.25
    n_meta = meta_tokens.shape[0]
    assert n_meta == N_META
    tm = min(ROW_TILE, seq)
    tq = min(ATTN_TILE, seq)
    assert seq % tm == 0 and seq % tq == 0

    tab_x = _rope_table(jnp.arange(n_meta, n_meta + seq))
    tab_m = _rope_table(jnp.arange(n_meta))
    zero_halo = jnp.zeros((SUBLANES, D_CONV), F32)

    hx = x.reshape(batch * seq, d)
    hm = meta_tokens.astype(x.dtype)
    for l in range(depth):
        ffn1 = (_pack_ffn_up(ffn1_w_up[l]), ffn1_w_down[l].astype(BF16))
        ffn2 = (_pack_ffn_up(ffn2_w_up[l]), ffn2_w_down[l].astype(BF16))
        mixw = _pack_mixer(mix_w_in[l], mix_b_gate[l], conv_w[l], q_norm_g[l], w_uq[l],
                           kv_norm_g[l], w_ukv[l], w_br_conv[l])
        wbrm = w_br_mla[l].astype(BF16)
        wo = w_o[l].astype(BF16)
        g = [ln_g[l, j][None, :] for j in range(3)]
        b = [ln_b[l, j][None, :] for j in range(3)]

        hm = _ffn_ln(hm, *ffn1, g[0], b[0], alpha=alpha, tm=n_meta)
        hx = _ffn_ln(hx, *ffn1, g[0], b[0], alpha=alpha, tm=tm)

        mconv_m, gmla_m, q_m, k_m, v_m, utail_m = _mixer_in(
            hm, tab_m, zero_halo, mixw, tm=n_meta, tiles_per_seq=1)
        mconv_x, gmla_x, q_x, k_x, v_x, _ = _mixer_in(
            hx, tab_x, utail_m, mixw, tm=tm, tiles_per_seq=seq // tm)

        o_x = _attention(q_x, k_x, v_x, k_m, v_m, batch=batch, seq=seq, tq=tq)
        hx = _mixer_out(hx, o_x, mconv_x, gmla_x, wbrm, wo, g[1], b[1], alpha=alpha, tm=tm)
        hx = _ffn_ln(hx, *ffn2, g[2], b[2], alpha=alpha, tm=tm)

        if l + 1 < depth:
            o_m = _attention_meta(q_m, k_m, v_m)
            hm = _mixer_out(hm, o_m, mconv_m, gmla_m, wbrm, wo, g[1], b[1], alpha=alpha, tm=n_meta)
            hm = _ffn_ln(hm, *ffn2, g[2], b[2], alpha=alpha, tm=n_meta)
    return hx.reshape(batch, seq, d)
```

```python
import functools
import math

import jax
import jax.numpy as jnp
from jax import lax
from jax.experimental import pallas as pl
from jax.experimental.pallas import tpu as pltpu

F32 = jnp.float32
BF16 = jnp.bfloat16

N_META = 16
CONV_WIDTH = 3
D_CONV = 512
MLA_HEADS = 8
QK_NOPE = 64
QK_ROPE = 32
V_HEAD = 64
Q_LORA = 256
KV_LORA = 128
ROPE_BASE = 10000.0
LN_EPS = 1e-5
RMS_EPS = 1e-6
NEG_INF = -1e30

LANES = 128
SUBLANES = 8
MXU_DIM = 256
VMEM_LIMIT_BYTES = 56 * 1024 * 1024

HEADS_PER_STEP = 2
ROW_TILE = 512
ATTN_QUERIES = 512
ATTN_KEYS = ATTN_QUERIES // 2
ATTN_STRIP = LANES

_C_B, _C_C, _C_H = 0, 512, 1024
_C_Q = 1536
_C_KV = 1792
_C_KR = 1920
_C_GC = 2048
_C_GM = 3072
D_IN_EXT = 4096


def _layer_norm(z, g, b):
    mu = jnp.mean(z, axis=-1, keepdims=True)
    d = z - mu
    var = jnp.mean(d * d, axis=-1, keepdims=True)
    return d * lax.rsqrt(var + LN_EPS) * g + b


def _rms_norm(z, g):
    return z * lax.rsqrt(jnp.mean(z * z, axis=-1, keepdims=True) + RMS_EPS) * g


def _dot(a, b):
    return jnp.dot(a, b, preferred_element_type=F32)


def _dot_nt(a, b):
    return lax.dot_general(a, b, (((1,), (1,)), ((), ())), preferred_element_type=F32)


def _resident(shape):
    nd = len(shape)
    return pl.BlockSpec(shape, lambda *_: (0,) * nd, pipeline_mode=pl.Buffered(1))


def _params(n_axes):
    return pltpu.CompilerParams(dimension_semantics=("arbitrary",) * n_axes,
                                vmem_limit_bytes=VMEM_LIMIT_BYTES)


def _ffn_ln_kernel(x_ref, wup_ref, wdown_ref, g_ref, b_ref, o_ref, act_ref, *, alpha):
    n_chunks, _, two_cw = wup_ref.shape
    cw = two_cw // 2
    x = x_ref[...]
    xb = x.astype(BF16)
    for c in range(n_chunks):
        h = _dot(xb, wup_ref[c])
        gate = h[:, :cw]
        up = h[:, cw:]
        act_ref[:, c * cw:(c + 1) * cw] = (gate * jax.nn.sigmoid(gate) * up).astype(BF16)
    y = _dot(act_ref[...], wdown_ref[...])
    o_ref[...] = _layer_norm(alpha * x + 0.5 * y, g_ref[...], b_ref[...])


def _ffn_ln(x, wup, wdown, g, b, *, alpha, tm):
    n, d = x.shape
    d_ff = wdown.shape[0]
    row = pl.BlockSpec((tm, d), lambda i: (i, 0))
    return pl.pallas_call(
        functools.partial(_ffn_ln_kernel, alpha=alpha),
        grid=(n // tm,),
        in_specs=[row, _resident(wup.shape), _resident(wdown.shape),
                  _resident(g.shape), _resident(b.shape)],
        out_specs=row,
        out_shape=jax.ShapeDtypeStruct((n, d), F32),
        scratch_shapes=[pltpu.VMEM((tm, d_ff), BF16)],
        compiler_params=_params(1),
        name="ffn_ln",
    )(x, wup, wdown, g, b)


def _mixer_in_kernel(h_ref, tab_ref, halo_ref, win_ref, convw_ref, bg_ref, qg_ref, wq_ref,
                     kvg_ref, wk_ref, wvt_ref, wbrc_ref,
                     mconv_ref, gmla_ref, q_ref, k_ref, vt_ref, utail_ref, ubuf_ref,
                     *, tiles_per_seq):
    tm = h_ref.shape[0]
    i = pl.program_id(0)
    hb = h_ref[...].astype(BF16)

    zc = _dot(hb, win_ref[:, _C_C:_C_C + 2 * D_CONV])
    u = zc[:, :D_CONV] * zc[:, D_CONV:]

    @pl.when(i % tiles_per_seq == 0)
    def _():
        ubuf_ref[0:SUBLANES, :] = halo_ref[...]

    @pl.when(i % tiles_per_seq != 0)
    def _():
        ubuf_ref[0:SUBLANES, :] = ubuf_ref[tm:tm + SUBLANES, :]

    ubuf_ref[SUBLANES:SUBLANES + tm, :] = u
    utail_ref[...] = u[tm - SUBLANES:, :]
    conv = convw_ref[CONV_WIDTH - 1:CONV_WIDTH, :] * u
    for k in range(CONV_WIDTH - 1):
        shift = CONV_WIDTH - 1 - k
        conv = conv + convw_ref[k:k + 1, :] * ubuf_ref[SUBLANES - shift:SUBLANES - shift + tm, :]
    zb = _dot(hb, win_ref[:, _C_B:_C_B + D_CONV])
    yc = _dot((zb * conv).astype(BF16), wbrc_ref[...])
    gc = _dot(hb, win_ref[:, _C_GC:_C_GC + yc.shape[1]])
    mconv_ref[...] = (jax.nn.sigmoid(gc + bg_ref[0:1, :]) * yc).astype(BF16)

    gm = _dot(hb, win_ref[:, _C_GM:_C_GM + yc.shape[1]])
    gmla_ref[...] = jax.nn.sigmoid(gm + bg_ref[1:2, :]).astype(BF16)

    cq = _rms_norm(_dot(hb, win_ref[:, _C_Q:_C_Q + Q_LORA]), qg_ref[...]).astype(BF16)
    q2 = _dot(cq, wq_ref[...])
    half = q2.shape[1] // 2
    tq_cos = tab_ref[:, 0:LANES]
    tq_sin = tab_ref[:, LANES:2 * LANES]
    for hd in range(MLA_HEADS):
        lo = hd * LANES
        q_ref[:, lo:lo + LANES] = (q2[:, lo:lo + LANES] * tq_cos
                                   + q2[:, half + lo:half + lo + LANES] * tq_sin).astype(BF16)

    zkv = _dot(hb, win_ref[:, _C_KV:_C_KV + 2 * LANES])
    ckv = _rms_norm(zkv[:, :KV_LORA], kvg_ref[...]).astype(BF16)
    krb = zkv[:, KV_LORA:]
    kr = (krb * tab_ref[:, 2 * LANES:3 * LANES]
          + pltpu.roll(krb, LANES - QK_ROPE, axis=1) * tab_ref[:, 3 * LANES:4 * LANES])
    kcat = jnp.concatenate([ckv, kr.astype(BF16)], axis=1)
    k_ref[...] = _dot(kcat, wk_ref[...]).astype(BF16)
    vt_ref[...] = _dot_nt(wvt_ref[...], ckv).astype(BF16)


def _mixer_in(h, tab, halo, w, *, tm, tiles_per_seq):
    n, d = h.shape
    row = lambda width: pl.BlockSpec((tm, width), lambda i: (i, 0))
    tab_spec = pl.BlockSpec((tm, tab.shape[1]), lambda i: (i % tiles_per_seq, 0))
    weights = [w["win"], w["convw"], w["bg"], w["qg"], w["wq"], w["kvg"], w["wk"], w["wvt"], w["wbrc"]]
    d_k = MLA_HEADS * LANES
    d_v = MLA_HEADS * V_HEAD
    return pl.pallas_call(
        functools.partial(_mixer_in_kernel, tiles_per_seq=tiles_per_seq),
        grid=(n // tm,),
        in_specs=[row(d), tab_spec, _resident(halo.shape)] + [_resident(a.shape) for a in weights],
        out_specs=[row(d), row(d), row(d_k), row(d_k), pl.BlockSpec((d_v, tm), lambda i: (0, i)),
                   pl.BlockSpec((SUBLANES, D_CONV), lambda i: (0, 0))],
        out_shape=[jax.ShapeDtypeStruct((n, d), BF16), jax.ShapeDtypeStruct((n, d), BF16),
                   jax.ShapeDtypeStruct((n, d_k), BF16), jax.ShapeDtypeStruct((n, d_k), BF16),
                   jax.ShapeDtypeStruct((d_v, n), BF16),
                   jax.ShapeDtypeStruct((SUBLANES, D_CONV), F32)],
        scratch_shapes=[pltpu.VMEM((tm + SUBLANES, D_CONV), F32)],
        compiler_params=_params(1),
        name="mixer_in",
    )(h, tab, halo, *weights)


def _values_with_ones(vt, hd):
    row = lax.broadcasted_iota(jnp.int32, vt.shape, 0)
    own = (row < V_HEAD) if hd == 0 else (row >= V_HEAD)
    return jnp.where(own, vt, jnp.ones_like(vt))


def _normalise_heads(acc0, acc1):
    o0 = acc0[:V_HEAD] / acc0[V_HEAD:V_HEAD + 1]
    o1 = acc1[V_HEAD:] / acc1[0:1]
    return jnp.concatenate([o0, o1], axis=0).T


def _attn_kernel(q_ref, k_ref, vt_ref, km_ref, vtm_ref, o_ref,
                 sa_ref, sb_ref, pa_ref, pb_ref, aa_ref, ab_ref, m_ref, acc_ref):
    tq = q_ref.shape[0]
    tk = sa_ref.shape[1]
    i = pl.program_id(2)
    heads = range(HEADS_PER_STEP)
    lanes = lambda hd: slice(hd * LANES, (hd + 1) * LANES)
    all_q = slice(0, tq)
    late_q = slice(tk, tq)

    for hd in heads:
        s0 = _dot_nt(km_ref[:, lanes(hd)], q_ref[:, lanes(hd)])
        m = jnp.max(s0, axis=0, keepdims=True)
        m_ref[hd] = m
        acc_ref[hd] = _dot(_values_with_ones(vtm_ref[...], hd), jnp.exp2(s0 - m).astype(BF16))

    def scores(blk, s_ref, cols):
        start = pl.multiple_of(blk * tk, tk)
        for hd in heads:
            s_ref[hd, :, cols] = _dot_nt(k_ref[pl.ds(start, tk), lanes(hd)], q_ref[cols, lanes(hd)])

    def weights(s_ref, p_ref, a_ref, cols, first_q=None):
        for hd in heads:
            for c0 in range(cols.start, cols.stop, ATTN_STRIP):
                strip = slice(c0, c0 + ATTN_STRIP)
                s = s_ref[hd, :, strip]
                if first_q is not None and c0 - first_q < tk:
                    key = lax.broadcasted_iota(jnp.int32, s.shape, 0)
                    qry = (c0 - first_q) + lax.broadcasted_iota(jnp.int32, s.shape, 1)
                    s = jnp.where(key <= qry, s, NEG_INF)
                m_old = m_ref[hd, :, strip]
                m_new = jnp.maximum(m_old, jnp.max(s, axis=0, keepdims=True))
                m_ref[hd, :, strip] = m_new
                a_ref[hd, :, strip] = jnp.exp2(m_old - m_new)
                p_ref[hd, :, strip] = jnp.exp2(s - m_new).astype(BF16)

    def accumulate(blk, p_ref, a_ref, cols):
        start = pl.multiple_of(blk * tk, tk)
        vt = vt_ref[:, pl.ds(start, tk)]
        for hd in heads:
            acc_ref[hd, :, cols] = (a_ref[hd, :, cols] * acc_ref[hd, :, cols]
                                    + _dot(_values_with_ones(vt, hd), p_ref[hd, :, cols]))

    scores(0, sa_ref, all_q)

    def block_pair(u, carry):
        scores(2 * u + 1, sb_ref, all_q)
        weights(sa_ref, pa_ref, aa_ref, all_q)
        accumulate(2 * u, pa_ref, aa_ref, all_q)
        scores(2 * u + 2, sa_ref, all_q)
        weights(sb_ref, pb_ref, ab_ref, all_q)
        accumulate(2 * u + 1, pb_ref, ab_ref, all_q)
        return carry

    lax.fori_loop(0, i, block_pair, 0)

    scores(2 * i + 1, sb_ref, late_q)
    weights(sa_ref, pa_ref, aa_ref, all_q, first_q=0)
    accumulate(2 * i, pa_ref, aa_ref, all_q)
    weights(sb_ref, pb_ref, ab_ref, late_q, first_q=tk)
    accumulate(2 * i + 1, pb_ref, ab_ref, late_q)

    o_ref[...] = _normalise_heads(acc_ref[0], acc_ref[1]).astype(BF16)


def _attention(q, k, vt, k_meta, vt_meta, *, batch, seq, tq):
    nq = seq // tq
    tk = tq // 2
    qw = HEADS_PER_STEP * LANES
    vw = HEADS_PER_STEP * V_HEAD
    n_meta = k_meta.shape[0]
    return pl.pallas_call(
        _attn_kernel,
        grid=(batch, MLA_HEADS // HEADS_PER_STEP, nq),
        in_specs=[pl.BlockSpec((tq, qw), lambda b, hp, i: (b * nq + i, hp)),
                  pl.BlockSpec((seq, qw), lambda b, hp, i: (b, hp)),
                  pl.BlockSpec((vw, seq), lambda b, hp, i: (hp, b)),
                  pl.BlockSpec((n_meta, qw), lambda b, hp, i: (0, hp)),
                  pl.BlockSpec((vw, n_meta), lambda b, hp, i: (hp, 0))],
        out_specs=pl.BlockSpec((tq, vw), lambda b, hp, i: (b * nq + i, hp)),
        out_shape=jax.ShapeDtypeStruct((batch * seq, MLA_HEADS * V_HEAD), BF16),
        scratch_shapes=[pltpu.VMEM((HEADS_PER_STEP, tk, tq), F32),
                        pltpu.VMEM((HEADS_PER_STEP, tk, tq), F32),
                        pltpu.VMEM((HEADS_PER_STEP, tk, tq), BF16),
                        pltpu.VMEM((HEADS_PER_STEP, tk, tq), BF16),
                        pltpu.VMEM((HEADS_PER_STEP, 1, tq), F32),
                        pltpu.VMEM((HEADS_PER_STEP, 1, tq), F32),
                        pltpu.VMEM((HEADS_PER_STEP, 1, tq), F32),
                        pltpu.VMEM((HEADS_PER_STEP, vw, tq), F32)],
        compiler_params=_params(3),
        name="attention",
    )(q, k, vt, k_meta, vt_meta)


def _attn_meta_kernel(q_ref, k_ref, vt_ref, o_ref):
    t = q_ref.shape[0]
    key = lax.broadcasted_iota(jnp.int32, (t, t), 0)
    qry = lax.broadcasted_iota(jnp.int32, (t, t), 1)
    accs = []
    for hd in range(HEADS_PER_STEP):
        lo = hd * LANES
        s = jnp.where(key <= qry, _dot_nt(k_ref[:, lo:lo + LANES], q_ref[:, lo:lo + LANES]), NEG_INF)
        p = jnp.exp2(s - jnp.max(s, axis=0, keepdims=True))
        accs.append(_dot(_values_with_ones(vt_ref[...], hd), p.astype(BF16)))
    o_ref[...] = _normalise_heads(*accs).astype(BF16)


def _attention_meta(q, k, vt):
    t = q.shape[0]
    qw = HEADS_PER_STEP * LANES
    vw = HEADS_PER_STEP * V_HEAD
    col = lambda width: pl.BlockSpec((t, width), lambda hp: (0, hp))
    return pl.pallas_call(
        _attn_meta_kernel,
        grid=(MLA_HEADS // HEADS_PER_STEP,),
        in_specs=[col(qw), col(qw), pl.BlockSpec((vw, t), lambda hp: (hp, 0))],
        out_specs=col(vw),
        out_shape=jax.ShapeDtypeStruct((t, MLA_HEADS * V_HEAD), BF16),
        compiler_params=_params(1),
        name="attention_meta",
    )(q, k, vt)


def _mixer_out_kernel(h_ref, o_ref, mconv_ref, gmla_ref, wbrm_ref, wo_ref, g_ref, b_ref, out_ref,
                      *, alpha):
    ym = _dot(o_ref[...], wbrm_ref[...])
    merged = mconv_ref[...].astype(F32) + gmla_ref[...].astype(F32) * ym
    mix = _dot(merged.astype(BF16), wo_ref[...])
    out_ref[...] = _layer_norm(alpha * h_ref[...] + mix, g_ref[...], b_ref[...])


def _mixer_out(h, o, mconv, gmla, wbrm, wo, g, b, *, alpha, tm):
    n, d = h.shape
    row = lambda width: pl.BlockSpec((tm, width), lambda i: (i, 0))
    return pl.pallas_call(
        functools.partial(_mixer_out_kernel, alpha=alpha),
        grid=(n // tm,),
        in_specs=[row(d), row(o.shape[1]), row(d), row(d), _resident(wbrm.shape),
                  _resident(wo.shape), _resident(g.shape), _resident(b.shape)],
        out_specs=row(d),
        out_shape=jax.ShapeDtypeStruct((n, d), F32),
        compiler_params=_params(1),
        name="mixer_out",
    )(h, o, mconv, gmla, wbrm, wo, g, b)


def _pack_ffn_up(w_up):
    d, two_ff = w_up.shape
    n_chunks = two_ff // 2 // MXU_DIM
    w = w_up.reshape(d, 2, n_chunks, MXU_DIM).transpose(2, 0, 1, 3)
    return w.reshape(n_chunks, d, 2 * MXU_DIM).astype(BF16)


def _pack_mixer(w_in, b_gate, conv_w, q_norm_g, w_uq, kv_norm_g, w_ukv, w_br_conv):
    d = w_in.shape[0]
    half_rope = QK_ROPE // 2
    c_kr = 3 * D_CONV + Q_LORA + KV_LORA
    kr = w_in[:, c_kr:c_kr + QK_ROPE]
    kr_swapped = jnp.concatenate([kr[:, half_rope:], kr[:, :half_rope]], axis=1)
    kr_block = jnp.concatenate([kr, kr_swapped, jnp.zeros((d, LANES - 2 * QK_ROPE), F32)], axis=1)
    win = jnp.concatenate([w_in[:, :c_kr], kr_block, w_in[:, c_kr + QK_ROPE:]], axis=1)
    assert win.shape[1] == D_IN_EXT

    wq = w_uq.reshape(Q_LORA, MLA_HEADS, QK_NOPE + QK_ROPE)
    nope, rope = wq[:, :, :QK_NOPE], wq[:, :, QK_NOPE:]
    rope_swapped = jnp.concatenate([rope[:, :, half_rope:], rope[:, :, :half_rope]], axis=2)
    pad = jnp.zeros((Q_LORA, MLA_HEADS, LANES - QK_NOPE - QK_ROPE), F32)
    qa = jnp.concatenate([nope, rope, pad], axis=2).reshape(Q_LORA, MLA_HEADS * LANES)
    qb = jnp.concatenate([jnp.zeros_like(nope), rope_swapped, pad], axis=2).reshape(Q_LORA, MLA_HEADS * LANES)

    wkv = w_ukv.reshape(KV_LORA, MLA_HEADS, QK_NOPE + V_HEAD)
    k_nope = jnp.concatenate([wkv[:, :, :QK_NOPE],
                              jnp.zeros((KV_LORA, MLA_HEADS, LANES - QK_NOPE), F32)], axis=2)
    place = jnp.concatenate([jnp.zeros((QK_ROPE, QK_NOPE), F32), jnp.eye(QK_ROPE, dtype=F32),
                             jnp.zeros((QK_ROPE, LANES - QK_NOPE - QK_ROPE), F32)], axis=1)
    wk = jnp.concatenate([k_nope.reshape(KV_LORA, MLA_HEADS * LANES),
                          jnp.tile(place, (1, MLA_HEADS)),
                          jnp.zeros((LANES - QK_ROPE, MLA_HEADS * LANES), F32)], axis=0)
    wv = wkv[:, :, QK_NOPE:].reshape(KV_LORA, MLA_HEADS * V_HEAD)
    return dict(win=win.astype(BF16), convw=conv_w, bg=b_gate, qg=q_norm_g[None, :],
                wq=jnp.concatenate([qa, qb], axis=1).astype(BF16), kvg=kv_norm_g[None, :],
                wk=wk.astype(BF16), wvt=wv.T.astype(BF16), wbrc=w_br_conv.astype(BF16))


def _rope_table(positions):
    inv_freq = 1.0 / (ROPE_BASE ** (jnp.arange(0, QK_ROPE, 2, dtype=F32) / QK_ROPE))
    ang = positions.astype(F32)[:, None] * inv_freq[None, :]
    cos, sin = jnp.cos(ang), jnp.sin(ang)
    t = positions.shape[0]
    c = (QK_NOPE + QK_ROPE) ** -0.5 * math.log2(math.e)
    zeros = lambda w: jnp.zeros((t, w), F32)
    pad = LANES - QK_NOPE - QK_ROPE
    q_cos = jnp.concatenate([jnp.full((t, QK_NOPE), c, F32), c * cos, c * cos, zeros(pad)], axis=1)
    q_sin = jnp.concatenate([zeros(QK_NOPE), -c * sin, c * sin, zeros(pad)], axis=1)
    k_cos = jnp.concatenate([cos, cos, zeros(LANES - QK_ROPE)], axis=1)
    k_sin = jnp.concatenate([-sin, sin, zeros(LANES - QK_ROPE)], axis=1)
    return jnp.concatenate([q_cos, q_sin, k_cos, k_sin], axis=1)


def kernel(x, meta_tokens, ffn1_w_up, ffn1_w_down, mix_w_in, mix_b_gate, conv_w, q_norm_g, w_uq,
           kv_norm_g, w_ukv, w_br_conv, w_br_mla, w_o, ffn2_w_up, ffn2_w_down, ln_g, ln_b):
    batch, seq, d = x.shape
    depth = ffn1_w_up.shape[0]
    alpha = (2 * depth) ** 0.25
    n_meta = meta_tokens.shape[0]
    assert n_meta == N_META
    tm = min(ROW_TILE, seq)
    tq = min(ATTN_QUERIES, seq)
    assert seq % tm == 0 and seq % tq == 0

    tab_x = _rope_table(jnp.arange(n_meta, n_meta + seq))
    tab_m = _rope_table(jnp.arange(n_meta))
    zero_halo = jnp.zeros((SUBLANES, D_CONV), F32)

    hx = x.reshape(batch * seq, d)
    hm = meta_tokens.astype(x.dtype)
    for l in range(depth):
        ffn1 = (_pack_ffn_up(ffn1_w_up[l]), ffn1_w_down[l].astype(BF16))
        ffn2 = (_pack_ffn_up(ffn2_w_up[l]), ffn2_w_down[l].astype(BF16))
        mixw = _pack_mixer(mix_w_in[l], mix_b_gate[l], conv_w[l], q_norm_g[l], w_uq[l],
                           kv_norm_g[l], w_ukv[l], w_br_conv[l])
        wbrm = w_br_mla[l].astype(BF16)
        wo = w_o[l].astype(BF16)
        g = [ln_g[l, j][None, :] for j in range(3)]
        b = [ln_b[l, j][None, :] for j in range(3)]

        hm = _ffn_ln(hm, *ffn1, g[0], b[0], alpha=alpha, tm=n_meta)
        hx = _ffn_ln(hx, *ffn1, g[0], b[0], alpha=alpha, tm=tm)

        mconv_m, gmla_m, q_m, k_m, vt_m, utail_m = _mixer_in(
            hm, tab_m, zero_halo, mixw, tm=n_meta, tiles_per_seq=1)
        mconv_x, gmla_x, q_x, k_x, vt_x, _ = _mixer_in(
            hx, tab_x, utail_m, mixw, tm=tm, tiles_per_seq=seq // tm)

        o_x = _attention(q_x, k_x, vt_x, k_m, vt_m, batch=batch, seq=seq, tq=tq)
        hx = _mixer_out(hx, o_x, mconv_x, gmla_x, wbrm, wo, g[1], b[1], alpha=alpha, tm=tm)
        hx = _ffn_ln(hx, *ffn2, g[2], b[2], alpha=alpha, tm=tm)

        if l + 1 < depth:
            o_m = _attention_meta(q_m, k_m, vt_m)
            hm = _mixer_out(hm, o_m, mconv_m, gmla_m, wbrm, wo, g[1], b[1], alpha=alpha, tm=n_meta)
            hm = _ffn_ln(hm, *ffn2, g[2], b[2], alpha=alpha, tm=n_meta)
    return hx.reshape(batch, seq, d)
```

```python
import functools
import math

import jax
import jax.numpy as jnp
from jax import lax
from jax.experimental import pallas as pl
from jax.experimental.pallas import tpu as pltpu

F32 = jnp.float32
BF16 = jnp.bfloat16

N_META = 16
CONV_WIDTH = 3
D_CONV = 512
MLA_HEADS = 8
QK_NOPE = 64
QK_ROPE = 32
V_HEAD = 64
Q_LORA = 256
KV_LORA = 128
ROPE_BASE = 10000.0
LN_EPS = 1e-5
RMS_EPS = 1e-6
NEG_INF = -1e30

LANES = 128
SUBLANES = 8
MXU_DIM = 256
VMEM_LIMIT_BYTES = 56 * 1024 * 1024

HEADS_PER_STEP = 2
ROW_TILE = 512
ATTN_QUERIES = 512
ATTN_KEYS = ATTN_QUERIES // 2
ATTN_STRIP = LANES

_C_B, _C_C, _C_H = 0, 512, 1024
_C_Q = 1536
_C_KV = 1792
_C_KR = 1920
_C_GC = 2048
_C_GM = 3072
D_IN_EXT = 4096


def _layer_norm(z, g, b):
    mu = jnp.mean(z, axis=-1, keepdims=True)
    d = z - mu
    var = jnp.mean(d * d, axis=-1, keepdims=True)
    return d * lax.rsqrt(var + LN_EPS) * g + b


def _rms_norm(z, g):
    return z * lax.rsqrt(jnp.mean(z * z, axis=-1, keepdims=True) + RMS_EPS) * g


def _dot(a, b):
    return jnp.dot(a, b, preferred_element_type=F32)


def _dot_nt(a, b):
    return lax.dot_general(a, b, (((1,), (1,)), ((), ())), preferred_element_type=F32)


def _resident(shape):
    nd = len(shape)
    return pl.BlockSpec(shape, lambda *_: (0,) * nd, pipeline_mode=pl.Buffered(1))


def _params(n_axes):
    return pltpu.CompilerParams(dimension_semantics=("arbitrary",) * n_axes,
                                vmem_limit_bytes=VMEM_LIMIT_BYTES)


def _ffn_ln_kernel(x_ref, wup_ref, wdown_ref, g_ref, b_ref, o_ref, act_ref, *, alpha):
    n_chunks, _, two_cw = wup_ref.shape
    cw = two_cw // 2
    x = x_ref[...]
    xb = x.astype(BF16)
    for c in range(n_chunks):
        h = _dot(xb, wup_ref[c])
        gate = h[:, :cw]
        up = h[:, cw:]
        act_ref[:, c * cw:(c + 1) * cw] = (gate * jax.nn.sigmoid(gate) * up).astype(BF16)
    y = _dot(act_ref[...], wdown_ref[...])
    o_ref[...] = _layer_norm(alpha * x + 0.5 * y, g_ref[...], b_ref[...])


def _ffn_ln(x, wup, wdown, g, b, *, alpha, tm):
    n, d = x.shape
    d_ff = wdown.shape[0]
    row = pl.BlockSpec((tm, d), lambda i: (i, 0))
    return pl.pallas_call(
        functools.partial(_ffn_ln_kernel, alpha=alpha),
        grid=(n // tm,),
        in_specs=[row, _resident(wup.shape), _resident(wdown.shape),
                  _resident(g.shape), _resident(b.shape)],
        out_specs=row,
        out_shape=jax.ShapeDtypeStruct((n, d), F32),
        scratch_shapes=[pltpu.VMEM((tm, d_ff), BF16)],
        compiler_params=_params(1),
        name="ffn_ln",
    )(x, wup, wdown, g, b)


def _mixer_in_kernel(h_ref, tab_ref, halo_ref, win_ref, convw_ref, bg_ref, qg_ref, wq_ref,
                     kvg_ref, wk_ref, wvt_ref, wbrc_ref,
                     mconv_ref, gmla_ref, q_ref, k_ref, vt_ref, utail_ref, ubuf_ref,
                     *, tiles_per_seq):
    tm = h_ref.shape[0]
    i = pl.program_id(0)
    hb = h_ref[...].astype(BF16)

    zc = _dot(hb, win_ref[:, _C_C:_C_C + 2 * D_CONV])
    u = zc[:, :D_CONV] * zc[:, D_CONV:]

    @pl.when(i % tiles_per_seq == 0)
    def _():
        ubuf_ref[0:SUBLANES, :] = halo_ref[...]

    @pl.when(i % tiles_per_seq != 0)
    def _():
        ubuf_ref[0:SUBLANES, :] = ubuf_ref[tm:tm + SUBLANES, :]

    ubuf_ref[SUBLANES:SUBLANES + tm, :] = u
    utail_ref[...] = u[tm - SUBLANES:, :]
    conv = convw_ref[CONV_WIDTH - 1:CONV_WIDTH, :] * u
    for k in range(CONV_WIDTH - 1):
        shift = CONV_WIDTH - 1 - k
        conv = conv + convw_ref[k:k + 1, :] * ubuf_ref[SUBLANES - shift:SUBLANES - shift + tm, :]
    zb = _dot(hb, win_ref[:, _C_B:_C_B + D_CONV])
    yc = _dot((zb * conv).astype(BF16), wbrc_ref[...])
    gc = _dot(hb, win_ref[:, _C_GC:_C_GC + yc.shape[1]])
    mconv_ref[...] = (jax.nn.sigmoid(gc + bg_ref[0:1, :]) * yc).astype(BF16)

    gm = _dot(hb, win_ref[:, _C_GM:_C_GM + yc.shape[1]])
    gmla_ref[...] = jax.nn.sigmoid(gm + bg_ref[1:2, :]).astype(BF16)

    cq = _rms_norm(_dot(hb, win_ref[:, _C_Q:_C_Q + Q_LORA]), qg_ref[...]).astype(BF16)
    q2 = _dot(cq, wq_ref[...])
    half = q2.shape[1] // 2
    tq_cos = tab_ref[:, 0:LANES]
    tq_sin = tab_ref[:, LANES:2 * LANES]
    for hd in range(MLA_HEADS):
        lo = hd * LANES
        q_ref[:, lo:lo + LANES] = (q2[:, lo:lo + LANES] * tq_cos
                                   + q2[:, half + lo:half + lo + LANES] * tq_sin).astype(BF16)

    zkv = _dot(hb, win_ref[:, _C_KV:_C_KV + 2 * LANES])
    ckv = _rms_norm(zkv[:, :KV_LORA], kvg_ref[...]).astype(BF16)
    krb = zkv[:, KV_LORA:]
    kr = (krb * tab_ref[:, 2 * LANES:3 * LANES]
          + pltpu.roll(krb, LANES - QK_ROPE, axis=1) * tab_ref[:, 3 * LANES:4 * LANES])
    kcat = jnp.concatenate([ckv, kr.astype(BF16)], axis=1)
    k_ref[...] = _dot(kcat, wk_ref[...]).astype(BF16)
    vt_ref[...] = _dot_nt(wvt_ref[...], ckv).astype(BF16)


def _mixer_in(h, tab, halo, w, *, tm, tiles_per_seq):
    n, d = h.shape
    row = lambda width: pl.BlockSpec((tm, width), lambda i: (i, 0))
    tab_spec = pl.BlockSpec((tm, tab.shape[1]), lambda i: (i % tiles_per_seq, 0))
    weights = [w["win"], w["convw"], w["bg"], w["qg"], w["wq"], w["kvg"], w["wk"], w["wvt"], w["wbrc"]]
    d_k = MLA_HEADS * LANES
    d_v = MLA_HEADS * V_HEAD
    return pl.pallas_call(
        functools.partial(_mixer_in_kernel, tiles_per_seq=tiles_per_seq),
        grid=(n // tm,),
        in_specs=[row(d), tab_spec, _resident(halo.shape)] + [_resident(a.shape) for a in weights],
        out_specs=[row(d), row(d), row(d_k), row(d_k), pl.BlockSpec((d_v, tm), lambda i: (0, i)),
                   pl.BlockSpec((SUBLANES, D_CONV), lambda i: (0, 0))],
        out_shape=[jax.ShapeDtypeStruct((n, d), BF16), jax.ShapeDtypeStruct((n, d), BF16),
                   jax.ShapeDtypeStruct((n, d_k), BF16), jax.ShapeDtypeStruct((n, d_k), BF16),
                   jax.ShapeDtypeStruct((d_v, n), BF16),
                   jax.ShapeDtypeStruct((SUBLANES, D_CONV), F32)],
        scratch_shapes=[pltpu.VMEM((tm + SUBLANES, D_CONV), F32)],
        compiler_params=_params(1),
        name="mixer_in",
    )(h, tab, halo, *weights)


ONES_ROWS = 16
ACC_ROWS = V_HEAD + ONES_ROWS


def _values_with_ones(vt, hd):
    own = vt[hd * V_HEAD:(hd + 1) * V_HEAD]
    return jnp.concatenate([own, jnp.ones((ONES_ROWS, vt.shape[1]), vt.dtype)], axis=0)


def _normalise_heads(acc0, acc1):
    o0 = acc0[:V_HEAD] / acc0[V_HEAD:V_HEAD + 1]
    o1 = acc1[:V_HEAD] / acc1[V_HEAD:V_HEAD + 1]
    return jnp.concatenate([o0, o1], axis=0).T


def _attn_kernel(q_ref, k_ref, vt_ref, km_ref, vtm_ref, o_ref,
                 qt_ref, sa_ref, sb_ref, pa_ref, pb_ref, aa_ref, ab_ref, m_ref, acc_ref):
    tq = q_ref.shape[0]
    n_strips, tk, sw = sa_ref.shape[1:]
    i = pl.program_id(2)
    heads = range(HEADS_PER_STEP)
    lanes = lambda hd: slice(hd * LANES, (hd + 1) * LANES)
    all_q = range(0, n_strips)
    late_q = range(n_strips // 2, n_strips)
    cols = lambda strips: slice(strips.start * sw, strips.stop * sw)

    for hd in heads:
        qt_ref[hd] = q_ref[:, lanes(hd)].T
        s0 = _dot(km_ref[:, lanes(hd)], qt_ref[hd])
        m = jnp.max(s0, axis=0, keepdims=True)
        m_ref[hd] = m
        acc_ref[hd] = _dot(_values_with_ones(vtm_ref[...], hd), jnp.exp2(s0 - m).astype(BF16))

    def scores(blk, s_ref, strips):
        start = pl.multiple_of(blk * tk, tk)
        for hd in heads:
            s = _dot(k_ref[pl.ds(start, tk), lanes(hd)], qt_ref[hd, :, cols(strips)])
            for c in strips:
                lo = (c - strips.start) * sw
                s_ref[hd, c] = s[:, lo:lo + sw]

    def weights(s_ref, p_ref, a_ref, strips, first_q=None):
        for hd in heads:
            for c in strips:
                strip = slice(c * sw, (c + 1) * sw)
                s = s_ref[hd, c]
                if first_q is not None and c * sw - first_q < tk:
                    key = lax.broadcasted_iota(jnp.int32, s.shape, 0)
                    qry = (c * sw - first_q) + lax.broadcasted_iota(jnp.int32, s.shape, 1)
                    s = jnp.where(key <= qry, s, NEG_INF)
                m_old = m_ref[hd, :, strip]
                m_new = jnp.maximum(m_old, jnp.max(s, axis=0, keepdims=True))
                m_ref[hd, :, strip] = m_new
                a_ref[hd, :, strip] = jnp.exp2(m_old - m_new)
                p_ref[hd, c] = jnp.exp2(s - m_new).astype(BF16)

    def accumulate(blk, p_ref, a_ref, strips):
        start = pl.multiple_of(blk * tk, tk)
        vt = vt_ref[:, pl.ds(start, tk)]
        for hd in heads:
            p = jnp.concatenate([p_ref[hd, c] for c in strips], axis=1)
            acc_ref[hd, :, cols(strips)] = (a_ref[hd, :, cols(strips)] * acc_ref[hd, :, cols(strips)]
                                            + _dot(_values_with_ones(vt, hd), p))

    pb_ref[...] = jnp.zeros(pb_ref.shape, BF16)
    ab_ref[...] = jnp.ones(ab_ref.shape, F32)
    scores(0, sa_ref, all_q)

    def block_pair(u):
        accumulate(jnp.maximum(2 * u - 1, 0), pb_ref, ab_ref, all_q)
        scores(2 * u + 1, sb_ref, all_q)
        weights(sa_ref, pa_ref, aa_ref, all_q)
        accumulate(2 * u, pa_ref, aa_ref, all_q)
        scores(2 * u + 2, sa_ref, all_q)
        weights(sb_ref, pb_ref, ab_ref, all_q)

    def two_block_pairs(v, carry):
        block_pair(2 * v)
        block_pair(2 * v + 1)
        return carry

    lax.fori_loop(0, lax.shift_right_logical(i, 1), two_block_pairs, 0)

    @pl.when((i & 1) == 1)
    def _():
        block_pair(i - 1)

    accumulate(jnp.maximum(2 * i - 1, 0), pb_ref, ab_ref, all_q)
    scores(2 * i + 1, sb_ref, late_q)
    weights(sa_ref, pa_ref, aa_ref, all_q, first_q=0)
    accumulate(2 * i, pa_ref, aa_ref, all_q)
    weights(sb_ref, pb_ref, ab_ref, late_q, first_q=tk)
    accumulate(2 * i + 1, pb_ref, ab_ref, late_q)

    o_ref[...] = _normalise_heads(acc_ref[0], acc_ref[1]).astype(BF16)


def _attention(q, k, vt, k_meta, vt_meta, *, batch, seq, tq):
    nq = seq // tq
    tk = tq // 2
    qw = HEADS_PER_STEP * LANES
    vw = HEADS_PER_STEP * V_HEAD
    n_meta = k_meta.shape[0]
    strips = (HEADS_PER_STEP, tq // ATTN_STRIP, tk, ATTN_STRIP)
    return pl.pallas_call(
        _attn_kernel,
        grid=(batch, MLA_HEADS // HEADS_PER_STEP, nq),
        in_specs=[pl.BlockSpec((tq, qw), lambda b, hp, i: (b * nq + i, hp)),
                  pl.BlockSpec((seq, qw), lambda b, hp, i: (b, hp)),
                  pl.BlockSpec((vw, seq), lambda b, hp, i: (hp, b)),
                  pl.BlockSpec((n_meta, qw), lambda b, hp, i: (0, hp)),
                  pl.BlockSpec((vw, n_meta), lambda b, hp, i: (hp, 0))],
        out_specs=pl.BlockSpec((tq, vw), lambda b, hp, i: (b * nq + i, hp)),
        out_shape=jax.ShapeDtypeStruct((batch * seq, MLA_HEADS * V_HEAD), BF16),
        scratch_shapes=[pltpu.VMEM((HEADS_PER_STEP, LANES, tq), BF16),
                        pltpu.VMEM(strips, F32),
                        pltpu.VMEM(strips, F32),
                        pltpu.VMEM(strips, BF16),
                        pltpu.VMEM(strips, BF16),
                        pltpu.VMEM((HEADS_PER_STEP, 1, tq), F32),
                        pltpu.VMEM((HEADS_PER_STEP, 1, tq), F32),
                        pltpu.VMEM((HEADS_PER_STEP, 1, tq), F32),
                        pltpu.VMEM((HEADS_PER_STEP, ACC_ROWS, tq), F32)],
        compiler_params=_params(3),
        name="attention",
    )(q, k, vt, k_meta, vt_meta)


def _attn_meta_kernel(q_ref, k_ref, vt_ref, o_ref):
    t = q_ref.shape[0]
    key = lax.broadcasted_iota(jnp.int32, (t, t), 0)
    qry = lax.broadcasted_iota(jnp.int32, (t, t), 1)
    accs = []
    for hd in range(HEADS_PER_STEP):
        lo = hd * LANES
        s = jnp.where(key <= qry, _dot_nt(k_ref[:, lo:lo + LANES], q_ref[:, lo:lo + LANES]), NEG_INF)
        p = jnp.exp2(s - jnp.max(s, axis=0, keepdims=True))
        accs.append(_dot(_values_with_ones(vt_ref[...], hd), p.astype(BF16)))
    o_ref[...] = _normalise_heads(*accs).astype(BF16)


def _attention_meta(q, k, vt):
    t = q.shape[0]
    qw = HEADS_PER_STEP * LANES
    vw = HEADS_PER_STEP * V_HEAD
    col = lambda width: pl.BlockSpec((t, width), lambda hp: (0, hp))
    return pl.pallas_call(
        _attn_meta_kernel,
        grid=(MLA_HEADS // HEADS_PER_STEP,),
        in_specs=[col(qw), col(qw), pl.BlockSpec((vw, t), lambda hp: (hp, 0))],
        out_specs=col(vw),
        out_shape=jax.ShapeDtypeStruct((t, MLA_HEADS * V_HEAD), BF16),
        compiler_params=_params(1),
        name="attention_meta",
    )(q, k, vt)


def _mixer_out_kernel(h_ref, o_ref, mconv_ref, gmla_ref, wbrm_ref, wo_ref, g_ref, b_ref, out_ref,
                      *, alpha):
    ym = _dot(o_ref[...], wbrm_ref[...])
    merged = mconv_ref[...].astype(F32) + gmla_ref[...].astype(F32) * ym
    mix = _dot(merged.astype(BF16), wo_ref[...])
    out_ref[...] = _layer_norm(alpha * h_ref[...] + mix, g_ref[...], b_ref[...])


def _mixer_out(h, o, mconv, gmla, wbrm, wo, g, b, *, alpha, tm):
    n, d = h.shape
    row = lambda width: pl.BlockSpec((tm, width), lambda i: (i, 0))
    return pl.pallas_call(
        functools.partial(_mixer_out_kernel, alpha=alpha),
        grid=(n // tm,),
        in_specs=[row(d), row(o.shape[1]), row(d), row(d), _resident(wbrm.shape),
                  _resident(wo.shape), _resident(g.shape), _resident(b.shape)],
        out_specs=row(d),
        out_shape=jax.ShapeDtypeStruct((n, d), F32),
        compiler_params=_params(1),
        name="mixer_out",
    )(h, o, mconv, gmla, wbrm, wo, g, b)


def _pack_ffn_up(w_up):
    d, two_ff = w_up.shape
    n_chunks = two_ff // 2 // MXU_DIM
    w = w_up.reshape(d, 2, n_chunks, MXU_DIM).transpose(2, 0, 1, 3)
    return w.reshape(n_chunks, d, 2 * MXU_DIM).astype(BF16)


def _pack_mixer(w_in, b_gate, conv_w, q_norm_g, w_uq, kv_norm_g, w_ukv, w_br_conv):
    d = w_in.shape[0]
    half_rope = QK_ROPE // 2
    c_kr = 3 * D_CONV + Q_LORA + KV_LORA
    kr = w_in[:, c_kr:c_kr + QK_ROPE]
    kr_swapped = jnp.concatenate([kr[:, half_rope:], kr[:, :half_rope]], axis=1)
    kr_block = jnp.concatenate([kr, kr_swapped, jnp.zeros((d, LANES - 2 * QK_ROPE), F32)], axis=1)
    win = jnp.concatenate([w_in[:, :c_kr], kr_block, w_in[:, c_kr + QK_ROPE:]], axis=1)
    assert win.shape[1] == D_IN_EXT

    wq = w_uq.reshape(Q_LORA, MLA_HEADS, QK_NOPE + QK_ROPE)
    nope, rope = wq[:, :, :QK_NOPE], wq[:, :, QK_NOPE:]
    rope_swapped = jnp.concatenate([rope[:, :, half_rope:], rope[:, :, :half_rope]], axis=2)
    pad = jnp.zeros((Q_LORA, MLA_HEADS, LANES - QK_NOPE - QK_ROPE), F32)
    qa = jnp.concatenate([nope, rope, pad], axis=2).reshape(Q_LORA, MLA_HEADS * LANES)
    qb = jnp.concatenate([jnp.zeros_like(nope), rope_swapped, pad], axis=2).reshape(Q_LORA, MLA_HEADS * LANES)

    wkv = w_ukv.reshape(KV_LORA, MLA_HEADS, QK_NOPE + V_HEAD)
    k_nope = jnp.concatenate([wkv[:, :, :QK_NOPE],
                              jnp.zeros((KV_LORA, MLA_HEADS, LANES - QK_NOPE), F32)], axis=2)
    place = jnp.concatenate([jnp.zeros((QK_ROPE, QK_NOPE), F32), jnp.eye(QK_ROPE, dtype=F32),
                             jnp.zeros((QK_ROPE, LANES - QK_NOPE - QK_ROPE), F32)], axis=1)
    wk = jnp.concatenate([k_nope.reshape(KV_LORA, MLA_HEADS * LANES),
                          jnp.tile(place, (1, MLA_HEADS)),
                          jnp.zeros((LANES - QK_ROPE, MLA_HEADS * LANES), F32)], axis=0)
    wv = wkv[:, :, QK_NOPE:].reshape(KV_LORA, MLA_HEADS * V_HEAD)
    return dict(win=win.astype(BF16), convw=conv_w, bg=b_gate, qg=q_norm_g[None, :],
                wq=jnp.concatenate([qa, qb], axis=1).astype(BF16), kvg=kv_norm_g[None, :],
                wk=wk.astype(BF16), wvt=wv.T.astype(BF16), wbrc=w_br_conv.astype(BF16))


def _rope_table(positions):
    inv_freq = 1.0 / (ROPE_BASE ** (jnp.arange(0, QK_ROPE, 2, dtype=F32) / QK_ROPE))
    ang = positions.astype(F32)[:, None] * inv_freq[None, :]
    cos, sin = jnp.cos(ang), jnp.sin(ang)
    t = positions.shape[0]
    c = (QK_NOPE + QK_ROPE) ** -0.5 * math.log2(math.e)
    zeros = lambda w: jnp.zeros((t, w), F32)
    pad = LANES - QK_NOPE - QK_ROPE
    q_cos = jnp.concatenate([jnp.full((t, QK_NOPE), c, F32), c * cos, c * cos, zeros(pad)], axis=1)
    q_sin = jnp.concatenate([zeros(QK_NOPE), -c * sin, c * sin, zeros(pad)], axis=1)
    k_cos = jnp.concatenate([cos, cos, zeros(LANES - QK_ROPE)], axis=1)
    k_sin = jnp.concatenate([-sin, sin, zeros(LANES - QK_ROPE)], axis=1)
    return jnp.concatenate([q_cos, q_sin, k_cos, k_sin], axis=1)


def kernel(x, meta_tokens, ffn1_w_up, ffn1_w_down, mix_w_in, mix_b_gate, conv_w, q_norm_g, w_uq,
           kv_norm_g, w_ukv, w_br_conv, w_br_mla, w_o, ffn2_w_up, ffn2_w_down, ln_g, ln_b):
    batch, seq, d = x.shape
    depth = ffn1_w_up.shape[0]
    alpha = (2 * depth) ** 0.25
    n_meta = meta_tokens.shape[0]
    assert n_meta == N_META
    tm = min(ROW_TILE, seq)
    tq = min(ATTN_QUERIES, seq)
    assert seq % tm == 0 and seq % tq == 0

    tab_x = _rope_table(jnp.arange(n_meta, n_meta + seq))
    tab_m = _rope_table(jnp.arange(n_meta))
    zero_halo = jnp.zeros((SUBLANES, D_CONV), F32)

    hx = x.reshape(batch * seq, d)
    hm = meta_tokens.astype(x.dtype)
    for l in range(depth):
        ffn1 = (_pack_ffn_up(ffn1_w_up[l]), ffn1_w_down[l].astype(BF16))
        ffn2 = (_pack_ffn_up(ffn2_w_up[l]), ffn2_w_down[l].astype(BF16))
        mixw = _pack_mixer(mix_w_in[l], mix_b_gate[l], conv_w[l], q_norm_g[l], w_uq[l],
                           kv_norm_g[l], w_ukv[l], w_br_conv[l])
        wbrm = w_br_mla[l].astype(BF16)
        wo = w_o[l].astype(BF16)
        g = [ln_g[l, j][None, :] for j in range(3)]
        b = [ln_b[l, j][None, :] for j in range(3)]

        hm = _ffn_ln(hm, *ffn1, g[0], b[0], alpha=alpha, tm=n_meta)
        hx = _ffn_ln(hx, *ffn1, g[0], b[0], alpha=alpha, tm=tm)

        mconv_m, gmla_m, q_m, k_m, vt_m, utail_m = _mixer_in(
            hm, tab_m, zero_halo, mixw, tm=n_meta, tiles_per_seq=1)
        mconv_x, gmla_x, q_x, k_x, vt_x, _ = _mixer_in(
            hx, tab_x, utail_m, mixw, tm=tm, tiles_per_seq=seq // tm)

        o_x = _attention(q_x, k_x, vt_x, k_m, vt_m, batch=batch, seq=seq, tq=tq)
        hx = _mixer_out(hx, o_x, mconv_x, gmla_x, wbrm, wo, g[1], b[1], alpha=alpha, tm=tm)
        hx = _ffn_ln(hx, *ffn2, g[2], b[2], alpha=alpha, tm=tm)

        if l + 1 < depth:
            o_m = _attention_meta(q_m, k_m, vt_m)
            hm = _mixer_out(hm, o_m, mconv_m, gmla_m, wbrm, wo, g[1], b[1], alpha=alpha, tm=n_meta)
            hm = _ffn_ln(hm, *ffn2, g[2], b[2], alpha=alpha, tm=n_meta)
    return hx.reshape(batch, seq, d)
```

```python
import functools
import math

import jax
import jax.numpy as jnp
from jax import lax
from jax.experimental import pallas as pl
from jax.experimental.pallas import tpu as pltpu

F32 = jnp.float32
BF16 = jnp.bfloat16

N_META = 16
CONV_WIDTH = 3
D_CONV = 512
MLA_HEADS = 8
QK_NOPE = 64
QK_ROPE = 32
V_HEAD = 64
Q_LORA = 256
KV_LORA = 128
ROPE_BASE = 10000.0
LN_EPS = 1e-5
RMS_EPS = 1e-6
NEG_INF = -1e30

LANES = 128
SUBLANES = 8
MXU_DIM = 256
VMEM_LIMIT_BYTES = 56 * 1024 * 1024

HEADS_PER_STEP = 2
ROW_TILE = 512
ATTN_QUERIES = 512
ATTN_KEYS = ATTN_QUERIES // 2
ATTN_STRIP = LANES

_C_B, _C_C, _C_H = 0, 512, 1024
_C_Q = 1536
_C_KV = 1792
_C_KR = 1920
_C_GC = 2048
_C_GM = 3072
D_IN_EXT = 4096


def _layer_norm(z, g, b):
    mu = jnp.mean(z, axis=-1, keepdims=True)
    d = z - mu
    var = jnp.mean(d * d, axis=-1, keepdims=True)
    return d * lax.rsqrt(var + LN_EPS) * g + b


def _rms_norm(z, g):
    return z * lax.rsqrt(jnp.mean(z * z, axis=-1, keepdims=True) + RMS_EPS) * g


def _dot(a, b):
    return jnp.dot(a, b, preferred_element_type=F32)


def _dot_nt(a, b):
    return lax.dot_general(a, b, (((1,), (1,)), ((), ())), preferred_element_type=F32)


def _resident(shape):
    nd = len(shape)
    return pl.BlockSpec(shape, lambda *_: (0,) * nd, pipeline_mode=pl.Buffered(1))


def _params(n_axes):
    return pltpu.CompilerParams(dimension_semantics=("arbitrary",) * n_axes,
                                vmem_limit_bytes=VMEM_LIMIT_BYTES)


def _ffn_ln_block(x, wup_ref, wdown_ref, g_ref, b_ref, act_ref, alpha):
    n_chunks, _, two_cw = wup_ref.shape
    cw = two_cw // 2
    xb = x.astype(BF16)
    for c in range(n_chunks):
        h = _dot(xb, wup_ref[c])
        gate = h[:, :cw]
        up = h[:, cw:]
        act_ref[:, c * cw:(c + 1) * cw] = (gate * jax.nn.sigmoid(gate) * up).astype(BF16)
    y = _dot(act_ref[...], wdown_ref[...])
    return _layer_norm(alpha * x + 0.5 * y, g_ref[...], b_ref[...])


def _ffn_ln_kernel(x_ref, wup_ref, wdown_ref, g_ref, b_ref, o_ref, act_ref, *, alpha):
    o_ref[...] = _ffn_ln_block(x_ref[...], wup_ref, wdown_ref, g_ref, b_ref, act_ref, alpha)


def _ffn_ln(x, wup, wdown, g, b, *, alpha, tm):
    n, d = x.shape
    d_ff = wdown.shape[0]
    row = pl.BlockSpec((tm, d), lambda i: (i, 0))
    return pl.pallas_call(
        functools.partial(_ffn_ln_kernel, alpha=alpha),
        grid=(n // tm,),
        in_specs=[row, _resident(wup.shape), _resident(wdown.shape),
                  _resident(g.shape), _resident(b.shape)],
        out_specs=row,
        out_shape=jax.ShapeDtypeStruct((n, d), F32),
        scratch_shapes=[pltpu.VMEM((tm, d_ff), BF16)],
        compiler_params=_params(1),
        name="ffn_ln",
    )(x, wup, wdown, g, b)


def _mixer_in_kernel(h_ref, tab_ref, halo_ref, win_ref, convw_ref, bg_ref, qg_ref, wq_ref,
                     kvg_ref, wk_ref, wvt_ref, wbrc_ref,
                     mconv_ref, gmla_ref, q_ref, k_ref, vt_ref, utail_ref, ubuf_ref,
                     *, tiles_per_seq):
    tm = h_ref.shape[0]
    i = pl.program_id(0)
    hb = h_ref[...].astype(BF16)

    d = h_ref.shape[1]
    zc = _dot(hb, win_ref[:, _C_C:_C_C + 2 * D_CONV])
    zb = _dot(hb, win_ref[:, _C_B:_C_B + D_CONV])
    zq = _dot(hb, win_ref[:, _C_Q:_C_Q + Q_LORA])
    zkv = _dot(hb, win_ref[:, _C_KV:_C_KV + 2 * LANES])
    gc = _dot(hb, win_ref[:, _C_GC:_C_GC + d])
    gm = _dot(hb, win_ref[:, _C_GM:_C_GM + d])

    u = zc[:, :D_CONV] * zc[:, D_CONV:]

    @pl.when(i % tiles_per_seq == 0)
    def _():
        ubuf_ref[0:SUBLANES, :] = halo_ref[...]

    @pl.when(i % tiles_per_seq != 0)
    def _():
        ubuf_ref[0:SUBLANES, :] = ubuf_ref[tm:tm + SUBLANES, :]

    ubuf_ref[SUBLANES:SUBLANES + tm, :] = u
    utail_ref[...] = u[tm - SUBLANES:, :]
    conv = convw_ref[CONV_WIDTH - 1:CONV_WIDTH, :] * u
    for k in range(CONV_WIDTH - 1):
        shift = CONV_WIDTH - 1 - k
        conv = conv + convw_ref[k:k + 1, :] * ubuf_ref[SUBLANES - shift:SUBLANES - shift + tm, :]
    yc = _dot((zb * conv).astype(BF16), wbrc_ref[...])
    mconv_ref[...] = (jax.nn.sigmoid(gc + bg_ref[0:1, :]) * yc).astype(BF16)
    gmla_ref[...] = jax.nn.sigmoid(gm + bg_ref[1:2, :]).astype(BF16)

    cq = _rms_norm(zq, qg_ref[...]).astype(BF16)
    q2 = _dot(cq, wq_ref[...])
    half = q2.shape[1] // 2
    tq_cos = tab_ref[:, 0:LANES]
    tq_sin = tab_ref[:, LANES:2 * LANES]
    for hd in range(MLA_HEADS):
        lo = hd * LANES
        q_ref[:, lo:lo + LANES] = (q2[:, lo:lo + LANES] * tq_cos
                                   + q2[:, half + lo:half + lo + LANES] * tq_sin).astype(BF16)

    ckv = _rms_norm(zkv[:, :KV_LORA], kvg_ref[...]).astype(BF16)
    krb = zkv[:, KV_LORA:]
    kr = (krb * tab_ref[:, 2 * LANES:3 * LANES]
          + pltpu.roll(krb, LANES - QK_ROPE, axis=1) * tab_ref[:, 3 * LANES:4 * LANES])
    kcat = jnp.concatenate([ckv, kr.astype(BF16)], axis=1)
    k_ref[...] = _dot(kcat, wk_ref[...]).astype(BF16)
    vt_ref[...] = _dot_nt(wvt_ref[...], ckv).astype(BF16)


def _mixer_in(h, tab, halo, w, *, tm, tiles_per_seq):
    n, d = h.shape
    row = lambda width: pl.BlockSpec((tm, width), lambda i: (i, 0))
    tab_spec = pl.BlockSpec((tm, tab.shape[1]), lambda i: (i % tiles_per_seq, 0))
    weights = [w["win"], w["convw"], w["bg"], w["qg"], w["wq"], w["kvg"], w["wk"], w["wvt"], w["wbrc"]]
    d_k = MLA_HEADS * LANES
    d_v = MLA_HEADS * V_HEAD
    return pl.pallas_call(
        functools.partial(_mixer_in_kernel, tiles_per_seq=tiles_per_seq),
        grid=(n // tm,),
        in_specs=[row(d), tab_spec, _resident(halo.shape)] + [_resident(a.shape) for a in weights],
        out_specs=[row(d), row(d), row(d_k), row(d_k), pl.BlockSpec((d_v, tm), lambda i: (0, i)),
                   pl.BlockSpec((SUBLANES, D_CONV), lambda i: (0, 0))],
        out_shape=[jax.ShapeDtypeStruct((n, d), BF16), jax.ShapeDtypeStruct((n, d), BF16),
                   jax.ShapeDtypeStruct((n, d_k), BF16), jax.ShapeDtypeStruct((n, d_k), BF16),
                   jax.ShapeDtypeStruct((d_v, n), BF16),
                   jax.ShapeDtypeStruct((SUBLANES, D_CONV), F32)],
        scratch_shapes=[pltpu.VMEM((tm + SUBLANES, D_CONV), F32)],
        compiler_params=_params(1),
        name="mixer_in",
    )(h, tab, halo, *weights)


ONES_ROWS = 16
ACC_ROWS = V_HEAD + ONES_ROWS


def _values_with_ones(vt, hd):
    own = vt[hd * V_HEAD:(hd + 1) * V_HEAD]
    return jnp.concatenate([own, jnp.ones((ONES_ROWS, vt.shape[1]), vt.dtype)], axis=0)


def _normalise_heads(acc0, acc1):
    o0 = acc0[:V_HEAD] / acc0[V_HEAD:V_HEAD + 1]
    o1 = acc1[:V_HEAD] / acc1[V_HEAD:V_HEAD + 1]
    return jnp.concatenate([o0, o1], axis=0).T


def _attn_kernel(q_ref, k_ref, vt_ref, km_ref, vtm_ref, o_ref,
                 qt_ref, sa_ref, sb_ref, pa_ref, pb_ref, aa_ref, ab_ref, m_ref, acc_ref):
    tq = q_ref.shape[0]
    n_strips, tk, sw = sa_ref.shape[1:]
    i = pl.program_id(2)
    heads = range(HEADS_PER_STEP)
    lanes = lambda hd: slice(hd * LANES, (hd + 1) * LANES)
    all_q = range(0, n_strips)
    late_q = range(n_strips // 2, n_strips)
    cols = lambda strips: slice(strips.start * sw, strips.stop * sw)

    for hd in heads:
        qt_ref[hd] = q_ref[:, lanes(hd)].T
        s0 = _dot(km_ref[:, lanes(hd)], qt_ref[hd])
        m = jnp.max(s0, axis=0, keepdims=True)
        m_ref[hd] = m
        acc_ref[hd] = _dot(_values_with_ones(vtm_ref[...], hd), jnp.exp2(s0 - m).astype(BF16))

    def scores(blk, s_ref, strips):
        start = pl.multiple_of(blk * tk, tk)
        for hd in heads:
            s = _dot(k_ref[pl.ds(start, tk), lanes(hd)], qt_ref[hd, :, cols(strips)])
            for c in strips:
                lo = (c - strips.start) * sw
                s_ref[hd, c] = s[:, lo:lo + sw]

    def weights(s_ref, p_ref, a_ref, strips, first_q=None):
        for hd in heads:
            for c in strips:
                strip = slice(c * sw, (c + 1) * sw)
                s = s_ref[hd, c]
                if first_q is not None and c * sw - first_q < tk:
                    key = lax.broadcasted_iota(jnp.int32, s.shape, 0)
                    qry = (c * sw - first_q) + lax.broadcasted_iota(jnp.int32, s.shape, 1)
                    s = jnp.where(key <= qry, s, NEG_INF)
                m_old = m_ref[hd, :, strip]
                m_new = jnp.maximum(m_old, jnp.max(s, axis=0, keepdims=True))
                m_ref[hd, :, strip] = m_new
                a_ref[hd, :, strip] = jnp.exp2(m_old - m_new)
                p_ref[hd, c] = jnp.exp2(s - m_new).astype(BF16)

    def accumulate(blk, p_ref, a_ref, strips):
        start = pl.multiple_of(blk * tk, tk)
        vt = vt_ref[:, pl.ds(start, tk)]
        for hd in heads:
            p = jnp.concatenate([p_ref[hd, c] for c in strips], axis=1)
            acc_ref[hd, :, cols(strips)] = (a_ref[hd, :, cols(strips)] * acc_ref[hd, :, cols(strips)]
                                            + _dot(_values_with_ones(vt, hd), p))

    pb_ref[...] = jnp.zeros(pb_ref.shape, BF16)
    ab_ref[...] = jnp.ones(ab_ref.shape, F32)
    scores(0, sa_ref, all_q)

    def block_pair(u):
        accumulate(jnp.maximum(2 * u - 1, 0), pb_ref, ab_ref, all_q)
        scores(2 * u + 1, sb_ref, all_q)
        weights(sa_ref, pa_ref, aa_ref, all_q)
        accumulate(2 * u, pa_ref, aa_ref, all_q)
        scores(2 * u + 2, sa_ref, all_q)
        weights(sb_ref, pb_ref, ab_ref, all_q)

    def two_block_pairs(v, carry):
        block_pair(2 * v)
        block_pair(2 * v + 1)
        return carry

    lax.fori_loop(0, lax.shift_right_logical(i, 1), two_block_pairs, 0)

    @pl.when((i & 1) == 1)
    def _():
        block_pair(i - 1)

    accumulate(jnp.maximum(2 * i - 1, 0), pb_ref, ab_ref, all_q)
    scores(2 * i + 1, sb_ref, late_q)
    weights(sa_ref, pa_ref, aa_ref, all_q, first_q=0)
    accumulate(2 * i, pa_ref, aa_ref, all_q)
    weights(sb_ref, pb_ref, ab_ref, late_q, first_q=tk)
    accumulate(2 * i + 1, pb_ref, ab_ref, late_q)

    o_ref[...] = _normalise_heads(acc_ref[0], acc_ref[1]).astype(BF16)


def _attention(q, k, vt, k_meta, vt_meta, *, batch, seq, tq):
    nq = seq // tq
    tk = tq // 2
    qw = HEADS_PER_STEP * LANES
    vw = HEADS_PER_STEP * V_HEAD
    n_meta = k_meta.shape[0]
    strips = (HEADS_PER_STEP, tq // ATTN_STRIP, tk, ATTN_STRIP)
    return pl.pallas_call(
        _attn_kernel,
        grid=(batch, MLA_HEADS // HEADS_PER_STEP, nq),
        in_specs=[pl.BlockSpec((tq, qw), lambda b, hp, i: (b * nq + i, hp)),
                  pl.BlockSpec((seq, qw), lambda b, hp, i: (b, hp)),
                  pl.BlockSpec((vw, seq), lambda b, hp, i: (hp, b)),
                  pl.BlockSpec((n_meta, qw), lambda b, hp, i: (0, hp)),
                  pl.BlockSpec((vw, n_meta), lambda b, hp, i: (hp, 0))],
        out_specs=pl.BlockSpec((tq, vw), lambda b, hp, i: (b * nq + i, hp)),
        out_shape=jax.ShapeDtypeStruct((batch * seq, MLA_HEADS * V_HEAD), BF16),
        scratch_shapes=[pltpu.VMEM((HEADS_PER_STEP, LANES, tq), BF16),
                        pltpu.VMEM(strips, F32),
                        pltpu.VMEM(strips, F32),
                        pltpu.VMEM(strips, BF16),
                        pltpu.VMEM(strips, BF16),
                        pltpu.VMEM((HEADS_PER_STEP, 1, tq), F32),
                        pltpu.VMEM((HEADS_PER_STEP, 1, tq), F32),
                        pltpu.VMEM((HEADS_PER_STEP, 1, tq), F32),
                        pltpu.VMEM((HEADS_PER_STEP, ACC_ROWS, tq), F32)],
        compiler_params=_params(3),
        name="attention",
    )(q, k, vt, k_meta, vt_meta)


def _attn_meta_kernel(q_ref, k_ref, vt_ref, o_ref):
    t = q_ref.shape[0]
    key = lax.broadcasted_iota(jnp.int32, (t, t), 0)
    qry = lax.broadcasted_iota(jnp.int32, (t, t), 1)
    accs = []
    for hd in range(HEADS_PER_STEP):
        lo = hd * LANES
        s = jnp.where(key <= qry, _dot_nt(k_ref[:, lo:lo + LANES], q_ref[:, lo:lo + LANES]), NEG_INF)
        p = jnp.exp2(s - jnp.max(s, axis=0, keepdims=True))
        accs.append(_dot(_values_with_ones(vt_ref[...], hd), p.astype(BF16)))
    o_ref[...] = _normalise_heads(*accs).astype(BF16)


def _attention_meta(q, k, vt):
    t = q.shape[0]
    qw = HEADS_PER_STEP * LANES
    vw = HEADS_PER_STEP * V_HEAD
    col = lambda width: pl.BlockSpec((t, width), lambda hp: (0, hp))
    return pl.pallas_call(
        _attn_meta_kernel,
        grid=(MLA_HEADS // HEADS_PER_STEP,),
        in_specs=[col(qw), col(qw), pl.BlockSpec((vw, t), lambda hp: (hp, 0))],
        out_specs=col(vw),
        out_shape=jax.ShapeDtypeStruct((t, MLA_HEADS * V_HEAD), BF16),
        compiler_params=_params(1),
        name="attention_meta",
    )(q, k, vt)


def _mixer_out_ffn_kernel(h_ref, o_ref, mconv_ref, gmla_ref, wbrm_ref, wo_ref, g1_ref, b1_ref,
                          wup_ref, wdown_ref, g2_ref, b2_ref, out_ref, act_ref, *, alpha):
    ym = _dot(o_ref[...], wbrm_ref[...])
    merged = mconv_ref[...].astype(F32) + gmla_ref[...].astype(F32) * ym
    mix = _dot(merged.astype(BF16), wo_ref[...])
    h2 = _layer_norm(alpha * h_ref[...] + mix, g1_ref[...], b1_ref[...])
    out_ref[...] = _ffn_ln_block(h2, wup_ref, wdown_ref, g2_ref, b2_ref, act_ref, alpha)


def _mixer_out_ffn(h, o, mconv, gmla, wbrm, wo, g1, b1, wup, wdown, g2, b2, *, alpha, tm):
    n, d = h.shape
    row = lambda width: pl.BlockSpec((tm, width), lambda i: (i, 0))
    weights = [wbrm, wo, g1, b1, wup, wdown, g2, b2]
    return pl.pallas_call(
        functools.partial(_mixer_out_ffn_kernel, alpha=alpha),
        grid=(n // tm,),
        in_specs=[row(d), row(o.shape[1]), row(d), row(d)] + [_resident(a.shape) for a in weights],
        out_specs=row(d),
        out_shape=jax.ShapeDtypeStruct((n, d), F32),
        scratch_shapes=[pltpu.VMEM((tm, wdown.shape[0]), BF16)],
        compiler_params=_params(1),
        name="mixer_out_ffn",
    )(h, o, mconv, gmla, *weights)


def _pack_ffn_up(w_up):
    d, two_ff = w_up.shape
    n_chunks = two_ff // 2 // MXU_DIM
    w = w_up.astype(BF16).reshape(d, 2, n_chunks, MXU_DIM).transpose(2, 0, 1, 3)
    return w.reshape(n_chunks, d, 2 * MXU_DIM)


def _pack_mixer(w_in, b_gate, conv_w, q_norm_g, w_uq, kv_norm_g, w_ukv, w_br_conv):
    d = w_in.shape[0]
    half_rope = QK_ROPE // 2
    c_kr = 3 * D_CONV + Q_LORA + KV_LORA
    w_in = w_in.astype(BF16)
    kr = w_in[:, c_kr:c_kr + QK_ROPE]
    kr_swapped = jnp.concatenate([kr[:, half_rope:], kr[:, :half_rope]], axis=1)
    kr_block = jnp.concatenate([kr, kr_swapped, jnp.zeros((d, LANES - 2 * QK_ROPE), BF16)], axis=1)
    win = jnp.concatenate([w_in[:, :c_kr], kr_block, w_in[:, c_kr + QK_ROPE:]], axis=1)
    assert win.shape[1] == D_IN_EXT

    wq = w_uq.astype(BF16).reshape(Q_LORA, MLA_HEADS, QK_NOPE + QK_ROPE)
    nope, rope = wq[:, :, :QK_NOPE], wq[:, :, QK_NOPE:]
    rope_swapped = jnp.concatenate([rope[:, :, half_rope:], rope[:, :, :half_rope]], axis=2)
    pad = jnp.zeros((Q_LORA, MLA_HEADS, LANES - QK_NOPE - QK_ROPE), BF16)
    qa = jnp.concatenate([nope, rope, pad], axis=2).reshape(Q_LORA, MLA_HEADS * LANES)
    qb = jnp.concatenate([jnp.zeros_like(nope), rope_swapped, pad], axis=2).reshape(Q_LORA, MLA_HEADS * LANES)

    wkv = w_ukv.astype(BF16).reshape(KV_LORA, MLA_HEADS, QK_NOPE + V_HEAD)
    k_nope = jnp.concatenate([wkv[:, :, :QK_NOPE],
                              jnp.zeros((KV_LORA, MLA_HEADS, LANES - QK_NOPE), BF16)], axis=2)
    place = jnp.concatenate([jnp.zeros((QK_ROPE, QK_NOPE), BF16), jnp.eye(QK_ROPE, dtype=BF16),
                             jnp.zeros((QK_ROPE, LANES - QK_NOPE - QK_ROPE), BF16)], axis=1)
    wk = jnp.concatenate([k_nope.reshape(KV_LORA, MLA_HEADS * LANES),
                          jnp.tile(place, (1, MLA_HEADS)),
                          jnp.zeros((LANES - QK_ROPE, MLA_HEADS * LANES), BF16)], axis=0)
    wv = wkv[:, :, QK_NOPE:].reshape(KV_LORA, MLA_HEADS * V_HEAD)
    return dict(win=win, convw=conv_w, bg=b_gate, qg=q_norm_g[None, :],
                wq=jnp.concatenate([qa, qb], axis=1), kvg=kv_norm_g[None, :],
                wk=wk, wvt=wv.T, wbrc=w_br_conv.astype(BF16))


def _rope_table(positions):
    inv_freq = 1.0 / (ROPE_BASE ** (jnp.arange(0, QK_ROPE, 2, dtype=F32) / QK_ROPE))
    ang = positions.astype(F32)[:, None] * inv_freq[None, :]
    cos, sin = jnp.cos(ang), jnp.sin(ang)
    t = positions.shape[0]
    c = (QK_NOPE + QK_ROPE) ** -0.5 * math.log2(math.e)
    zeros = lambda w: jnp.zeros((t, w), F32)
    pad = LANES - QK_NOPE - QK_ROPE
    q_cos = jnp.concatenate([jnp.full((t, QK_NOPE), c, F32), c * cos, c * cos, zeros(pad)], axis=1)
    q_sin = jnp.concatenate([zeros(QK_NOPE), -c * sin, c * sin, zeros(pad)], axis=1)
    k_cos = jnp.concatenate([cos, cos, zeros(LANES - QK_ROPE)], axis=1)
    k_sin = jnp.concatenate([-sin, sin, zeros(LANES - QK_ROPE)], axis=1)
    return jnp.concatenate([q_cos, q_sin, k_cos, k_sin], axis=1)


def kernel(x, meta_tokens, ffn1_w_up, ffn1_w_down, mix_w_in, mix_b_gate, conv_w, q_norm_g, w_uq,
           kv_norm_g, w_ukv, w_br_conv, w_br_mla, w_o, ffn2_w_up, ffn2_w_down, ln_g, ln_b):
    batch, seq, d = x.shape
    depth = ffn1_w_up.shape[0]
    alpha = (2 * depth) ** 0.25
    n_meta = meta_tokens.shape[0]
    assert n_meta == N_META
    tm = min(ROW_TILE, seq)
    tq = min(ATTN_QUERIES, seq)
    assert seq % tm == 0 and seq % tq == 0

    tab_x = _rope_table(jnp.arange(n_meta, n_meta + seq))
    tab_m = _rope_table(jnp.arange(n_meta))
    zero_halo = jnp.zeros((SUBLANES, D_CONV), F32)

    hx = x.reshape(batch * seq, d)
    hm = meta_tokens.astype(x.dtype)
    for l in range(depth):
        ffn1 = (_pack_ffn_up(ffn1_w_up[l]), ffn1_w_down[l].astype(BF16))
        ffn2 = (_pack_ffn_up(ffn2_w_up[l]), ffn2_w_down[l].astype(BF16))
        mixw = _pack_mixer(mix_w_in[l], mix_b_gate[l], conv_w[l], q_norm_g[l], w_uq[l],
                           kv_norm_g[l], w_ukv[l], w_br_conv[l])
        wbrm = w_br_mla[l].astype(BF16)
        wo = w_o[l].astype(BF16)
        g = [ln_g[l, j][None, :] for j in range(3)]
        b = [ln_b[l, j][None, :] for j in range(3)]

        hm = _ffn_ln(hm, *ffn1, g[0], b[0], alpha=alpha, tm=n_meta)
        hx = _ffn_ln(hx, *ffn1, g[0], b[0], alpha=alpha, tm=tm)

        mconv_m, gmla_m, q_m, k_m, vt_m, utail_m = _mixer_in(
            hm, tab_m, zero_halo, mixw, tm=n_meta, tiles_per_seq=1)
        mconv_x, gmla_x, q_x, k_x, vt_x, _ = _mixer_in(
            hx, tab_x, utail_m, mixw, tm=tm, tiles_per_seq=seq // tm)

        o_x = _attention(q_x, k_x, vt_x, k_m, vt_m, batch=batch, seq=seq, tq=tq)
        hx = _mixer_out_ffn(hx, o_x, mconv_x, gmla_x, wbrm, wo, g[1], b[1], *ffn2, g[2], b[2],
                            alpha=alpha, tm=tm)

        if l + 1 < depth:
            o_m = _attention_meta(q_m, k_m, vt_m)
            hm = _mixer_out_ffn(hm, o_m, mconv_m, gmla_m, wbrm, wo, g[1], b[1], *ffn2, g[2], b[2],
                                alpha=alpha, tm=n_meta)
    return hx.reshape(batch, seq, d)
```

```python
import functools
import math

import jax
import jax.numpy as jnp
from jax import lax
from jax.experimental import pallas as pl
from jax.experimental.pallas import tpu as pltpu

F32 = jnp.float32
BF16 = jnp.bfloat16

N_META = 16
CONV_WIDTH = 3
D_CONV = 512
MLA_HEADS = 8
QK_NOPE = 64
QK_ROPE = 32
V_HEAD = 64
Q_LORA = 256
KV_LORA = 128
ROPE_BASE = 10000.0
LN_EPS = 1e-5
RMS_EPS = 1e-6
NEG_INF = -1e30

LANES = 128
SUBLANES = 8
MXU_DIM = 256
VMEM_LIMIT_BYTES = 56 * 1024 * 1024

HEADS_PER_STEP = 2
ROW_TILE = 512
ATTN_QUERIES = 512
ATTN_KEYS = ATTN_QUERIES // 2
ATTN_STRIP = LANES

_C_B, _C_C, _C_H = 0, 512, 1024
_C_Q = 1536
_C_KV = 1792
_C_KR = 1920
_C_GC = 2048
_C_GM = 3072
D_IN_EXT = 4096


def _layer_norm(z, g, b):
    mu = jnp.mean(z, axis=-1, keepdims=True)
    d = z - mu
    var = jnp.mean(d * d, axis=-1, keepdims=True)
    return d * lax.rsqrt(var + LN_EPS) * g + b


def _rms_norm(z, g):
    return z * lax.rsqrt(jnp.mean(z * z, axis=-1, keepdims=True) + RMS_EPS) * g


def _dot(a, b):
    return jnp.dot(a, b, preferred_element_type=F32)


def _dot_nt(a, b):
    return lax.dot_general(a, b, (((1,), (1,)), ((), ())), preferred_element_type=F32)


def _resident(shape):
    nd = len(shape)
    return pl.BlockSpec(shape, lambda *_: (0,) * nd, pipeline_mode=pl.Buffered(1))


def _params(n_axes):
    return pltpu.CompilerParams(dimension_semantics=("arbitrary",) * n_axes,
                                vmem_limit_bytes=VMEM_LIMIT_BYTES)


def _ffn_ln_block(x, wup_ref, wdown_ref, g_ref, b_ref, act_ref, alpha):
    d_ff = wdown_ref.shape[0]
    cw = MXU_DIM
    xb = x.astype(BF16)
    for c in range(d_ff // cw):
        w = jnp.concatenate([wup_ref[:, c * cw:(c + 1) * cw],
                             wup_ref[:, d_ff + c * cw:d_ff + (c + 1) * cw]], axis=1)
        h = _dot(xb, w)
        gate = h[:, :cw]
        up = h[:, cw:]
        act_ref[:, c * cw:(c + 1) * cw] = (gate * jax.nn.sigmoid(gate) * up).astype(BF16)
    y = _dot(act_ref[...], wdown_ref[...])
    return _layer_norm(alpha * x + 0.5 * y, g_ref[...], b_ref[...])


def _ffn_ln_kernel(x_ref, wup_ref, wdown_ref, g_ref, b_ref, o_ref, act_ref, *, alpha):
    o_ref[...] = _ffn_ln_block(x_ref[...], wup_ref, wdown_ref, g_ref, b_ref, act_ref, alpha)


def _ffn_ln(x, wup, wdown, g, b, *, alpha, tm):
    n, d = x.shape
    d_ff = wdown.shape[0]
    row = pl.BlockSpec((tm, d), lambda i: (i, 0))
    return pl.pallas_call(
        functools.partial(_ffn_ln_kernel, alpha=alpha),
        grid=(n // tm,),
        in_specs=[row, _resident(wup.shape), _resident(wdown.shape),
                  _resident(g.shape), _resident(b.shape)],
        out_specs=row,
        out_shape=jax.ShapeDtypeStruct((n, d), F32),
        scratch_shapes=[pltpu.VMEM((tm, d_ff), BF16)],
        compiler_params=_params(1),
        name="ffn_ln",
    )(x, wup, wdown, g, b)


def _mixer_in_kernel(h_ref, tab_ref, halo_ref, win_ref, convw_ref, bg_ref, qg_ref, wq_ref,
                     kvg_ref, wk_ref, wvt_ref, wbrc_ref,
                     mconv_ref, gmla_ref, q_ref, k_ref, vt_ref, utail_ref, ubuf_ref,
                     *, tiles_per_seq):
    tm = h_ref.shape[0]
    i = pl.program_id(0)
    hb = h_ref[...].astype(BF16)

    d = h_ref.shape[1]
    zc = _dot(hb, win_ref[:, _C_C:_C_C + 2 * D_CONV])
    zb = _dot(hb, win_ref[:, _C_B:_C_B + D_CONV])
    zq = _dot(hb, win_ref[:, _C_Q:_C_Q + Q_LORA])
    zkv = _dot(hb, win_ref[:, _C_KV:_C_KV + 2 * LANES])
    gc = _dot(hb, win_ref[:, _C_GC:_C_GC + d])
    gm = _dot(hb, win_ref[:, _C_GM:_C_GM + d])

    u = zc[:, :D_CONV] * zc[:, D_CONV:]

    @pl.when(i % tiles_per_seq == 0)
    def _():
        ubuf_ref[0:SUBLANES, :] = halo_ref[...]

    @pl.when(i % tiles_per_seq != 0)
    def _():
        ubuf_ref[0:SUBLANES, :] = ubuf_ref[tm:tm + SUBLANES, :]

    ubuf_ref[SUBLANES:SUBLANES + tm, :] = u
    utail_ref[...] = u[tm - SUBLANES:, :]
    conv = convw_ref[CONV_WIDTH - 1:CONV_WIDTH, :] * u
    for k in range(CONV_WIDTH - 1):
        shift = CONV_WIDTH - 1 - k
        conv = conv + convw_ref[k:k + 1, :] * ubuf_ref[SUBLANES - shift:SUBLANES - shift + tm, :]
    yc = _dot((zb * conv).astype(BF16), wbrc_ref[...])
    mconv_ref[...] = (jax.nn.sigmoid(gc + bg_ref[0:1, :]) * yc).astype(BF16)
    gmla_ref[...] = jax.nn.sigmoid(gm + bg_ref[1:2, :]).astype(BF16)

    cq = _rms_norm(zq, qg_ref[...]).astype(BF16)
    q2 = _dot(cq, wq_ref[...])
    half = q2.shape[1] // 2
    tq_cos = tab_ref[:, 0:LANES]
    tq_sin = tab_ref[:, LANES:2 * LANES]
    for hd in range(MLA_HEADS):
        lo = hd * LANES
        q_ref[:, lo:lo + LANES] = (q2[:, lo:lo + LANES] * tq_cos
                                   + q2[:, half + lo:half + lo + LANES] * tq_sin).astype(BF16)

    ckv = _rms_norm(zkv[:, :KV_LORA], kvg_ref[...]).astype(BF16)
    krb = zkv[:, KV_LORA:]
    kr = (krb * tab_ref[:, 2 * LANES:3 * LANES]
          + pltpu.roll(krb, LANES - QK_ROPE, axis=1) * tab_ref[:, 3 * LANES:4 * LANES])
    kcat = jnp.concatenate([ckv, kr.astype(BF16)], axis=1)
    k_ref[...] = _dot(kcat, wk_ref[...]).astype(BF16)
    vt_ref[...] = _dot_nt(wvt_ref[...], ckv).astype(BF16)


def _mixer_in(h, tab, halo, w, *, tm, tiles_per_seq):
    n, d = h.shape
    row = lambda width: pl.BlockSpec((tm, width), lambda i: (i, 0))
    tab_spec = pl.BlockSpec((tm, tab.shape[1]), lambda i: (i % tiles_per_seq, 0))
    weights = [w["win"], w["convw"], w["bg"], w["qg"], w["wq"], w["kvg"], w["wk"], w["wvt"], w["wbrc"]]
    d_k = MLA_HEADS * LANES
    d_v = MLA_HEADS * V_HEAD
    return pl.pallas_call(
        functools.partial(_mixer_in_kernel, tiles_per_seq=tiles_per_seq),
        grid=(n // tm,),
        in_specs=[row(d), tab_spec, _resident(halo.shape)] + [_resident(a.shape) for a in weights],
        out_specs=[row(d), row(d), row(d_k), row(d_k), pl.BlockSpec((d_v, tm), lambda i: (0, i)),
                   pl.BlockSpec((SUBLANES, D_CONV), lambda i: (0, 0))],
        out_shape=[jax.ShapeDtypeStruct((n, d), BF16), jax.ShapeDtypeStruct((n, d), BF16),
                   jax.ShapeDtypeStruct((n, d_k), BF16), jax.ShapeDtypeStruct((n, d_k), BF16),
                   jax.ShapeDtypeStruct((d_v, n), BF16),
                   jax.ShapeDtypeStruct((SUBLANES, D_CONV), F32)],
        scratch_shapes=[pltpu.VMEM((tm + SUBLANES, D_CONV), F32)],
        compiler_params=_params(1),
        name="mixer_in",
    )(h, tab, halo, *weights)


ONES_ROWS = 16
ACC_ROWS = V_HEAD + ONES_ROWS


def _values_with_ones(vt, hd):
    own = vt[hd * V_HEAD:(hd + 1) * V_HEAD]
    return jnp.concatenate([own, jnp.ones((ONES_ROWS, vt.shape[1]), vt.dtype)], axis=0)


def _normalise_heads(acc0, acc1):
    o0 = acc0[:V_HEAD] / acc0[V_HEAD:V_HEAD + 1]
    o1 = acc1[:V_HEAD] / acc1[V_HEAD:V_HEAD + 1]
    return jnp.concatenate([o0, o1], axis=0).T


def _attn_kernel(q_ref, k_ref, vt_ref, km_ref, vtm_ref, o_ref,
                 qt_ref, sa_ref, sb_ref, pa_ref, pb_ref, aa_ref, ab_ref, m_ref, acc_ref):
    tq = q_ref.shape[0]
    n_strips, _, sw = sa_ref.shape[1:]
    tk = tq // 2
    i = pl.program_id(2)
    heads = range(HEADS_PER_STEP)
    lanes = lambda hd: slice(hd * LANES, (hd + 1) * LANES)
    all_q = range(0, n_strips)
    late_q = range(n_strips // 2, n_strips)
    cols = lambda strips: slice(strips.start * sw, strips.stop * sw)

    for hd in heads:
        qt_ref[hd] = q_ref[:, lanes(hd)].T
        s0 = _dot(km_ref[:, lanes(hd)], qt_ref[hd])
        m = jnp.max(s0, axis=0, keepdims=True)
        m_ref[hd] = m
        acc_ref[hd] = _dot(_values_with_ones(vtm_ref[...], hd), jnp.exp2(s0 - m).astype(BF16))

    def scores(blk, s_ref, strips):
        start = pl.multiple_of(blk * tk, tk)
        for hd in heads:
            s = _dot(k_ref[pl.ds(start, tk), lanes(hd)], qt_ref[hd, :, cols(strips)])
            for c in strips:
                lo = (c - strips.start) * sw
                s_ref[hd, c, 0:tk] = s[:, lo:lo + sw]

    def weights(s_ref, p_ref, a_ref, strips, first_q=None):
        for hd in heads:
            for c in strips:
                strip = slice(c * sw, (c + 1) * sw)
                s = s_ref[hd, c, 0:tk]
                if first_q is not None and c * sw - first_q < tk:
                    key = lax.broadcasted_iota(jnp.int32, s.shape, 0)
                    qry = (c * sw - first_q) + lax.broadcasted_iota(jnp.int32, s.shape, 1)
                    s = jnp.where(key <= qry, s, NEG_INF)
                m_old = m_ref[hd, :, strip]
                m_new = jnp.maximum(m_old, jnp.max(s, axis=0, keepdims=True))
                m_ref[hd, :, strip] = m_new
                a_ref[hd, :, strip] = jnp.exp2(m_old - m_new)
                p_ref[hd, c, 0:tk] = jnp.exp2(s - m_new).astype(BF16)

    def accumulate(blk, p_ref, a_ref, strips):
        start = pl.multiple_of(blk * tk, tk)
        vt = vt_ref[:, pl.ds(start, tk)]
        for hd in heads:
            p = jnp.concatenate([p_ref[hd, c, 0:tk] for c in strips], axis=1)
            acc_ref[hd, :, cols(strips)] = (a_ref[hd, :, cols(strips)] * acc_ref[hd, :, cols(strips)]
                                            + _dot(_values_with_ones(vt, hd), p))

    pb_ref[...] = jnp.zeros(pb_ref.shape, BF16)
    ab_ref[...] = jnp.ones(ab_ref.shape, F32)
    scores(0, sa_ref, all_q)

    def block_pair(u):
        accumulate(jnp.maximum(2 * u - 1, 0), pb_ref, ab_ref, all_q)
        scores(2 * u + 1, sb_ref, all_q)
        weights(sa_ref, pa_ref, aa_ref, all_q)
        accumulate(2 * u, pa_ref, aa_ref, all_q)
        scores(2 * u + 2, sa_ref, all_q)
        weights(sb_ref, pb_ref, ab_ref, all_q)

    def two_block_pairs(v, carry):
        block_pair(2 * v)
        block_pair(2 * v + 1)
        return carry

    lax.fori_loop(0, lax.shift_right_logical(i, 1), two_block_pairs, 0)

    @pl.when((i & 1) == 1)
    def _():
        block_pair(i - 1)

    accumulate(jnp.maximum(2 * i - 1, 0), pb_ref, ab_ref, all_q)
    scores(2 * i + 1, sb_ref, late_q)
    weights(sa_ref, pa_ref, aa_ref, all_q, first_q=0)
    accumulate(2 * i, pa_ref, aa_ref, all_q)
    weights(sb_ref, pb_ref, ab_ref, late_q, first_q=tk)
    accumulate(2 * i + 1, pb_ref, ab_ref, late_q)

    o_ref[...] = _normalise_heads(acc_ref[0], acc_ref[1]).astype(BF16)


def _attention(q, k, vt, k_meta, vt_meta, *, batch, seq, tq):
    nq = seq // tq
    tk = tq // 2
    qw = HEADS_PER_STEP * LANES
    vw = HEADS_PER_STEP * V_HEAD
    n_meta = k_meta.shape[0]
    s_strips = (HEADS_PER_STEP, tq // ATTN_STRIP, tk + SUBLANES, ATTN_STRIP)
    p_strips = (HEADS_PER_STEP, tq // ATTN_STRIP, tk + 2 * SUBLANES, ATTN_STRIP)
    return pl.pallas_call(
        _attn_kernel,
        grid=(batch, MLA_HEADS // HEADS_PER_STEP, nq),
        in_specs=[pl.BlockSpec((tq, qw), lambda b, hp, i: (b * nq + i, hp)),
                  pl.BlockSpec((seq, qw), lambda b, hp, i: (b, hp)),
                  pl.BlockSpec((vw, seq), lambda b, hp, i: (hp, b)),
                  pl.BlockSpec((n_meta, qw), lambda b, hp, i: (0, hp)),
                  pl.BlockSpec((vw, n_meta), lambda b, hp, i: (hp, 0))],
        out_specs=pl.BlockSpec((tq, vw), lambda b, hp, i: (b * nq + i, hp)),
        out_shape=jax.ShapeDtypeStruct((batch * seq, MLA_HEADS * V_HEAD), BF16),
        scratch_shapes=[pltpu.VMEM((HEADS_PER_STEP, LANES, tq), BF16),
                        pltpu.VMEM(s_strips, F32),
                        pltpu.VMEM(s_strips, F32),
                        pltpu.VMEM(p_strips, BF16),
                        pltpu.VMEM(p_strips, BF16),
                        pltpu.VMEM((HEADS_PER_STEP, 1, tq), F32),
                        pltpu.VMEM((HEADS_PER_STEP, 1, tq), F32),
                        pltpu.VMEM((HEADS_PER_STEP, 1, tq), F32),
                        pltpu.VMEM((HEADS_PER_STEP, ACC_ROWS, tq), F32)],
        compiler_params=_params(3),
        name="attention",
    )(q, k, vt, k_meta, vt_meta)


def _attn_meta_kernel(q_ref, k_ref, vt_ref, o_ref):
    t = q_ref.shape[0]
    key = lax.broadcasted_iota(jnp.int32, (t, t), 0)
    qry = lax.broadcasted_iota(jnp.int32, (t, t), 1)
    accs = []
    for hd in range(HEADS_PER_STEP):
        lo = hd * LANES
        s = jnp.where(key <= qry, _dot_nt(k_ref[:, lo:lo + LANES], q_ref[:, lo:lo + LANES]), NEG_INF)
        p = jnp.exp2(s - jnp.max(s, axis=0, keepdims=True))
        accs.append(_dot(_values_with_ones(vt_ref[...], hd), p.astype(BF16)))
    o_ref[...] = _normalise_heads(*accs).astype(BF16)


def _attention_meta(q, k, vt):
    t = q.shape[0]
    qw = HEADS_PER_STEP * LANES
    vw = HEADS_PER_STEP * V_HEAD
    col = lambda width: pl.BlockSpec((t, width), lambda hp: (0, hp))
    return pl.pallas_call(
        _attn_meta_kernel,
        grid=(MLA_HEADS // HEADS_PER_STEP,),
        in_specs=[col(qw), col(qw), pl.BlockSpec((vw, t), lambda hp: (hp, 0))],
        out_specs=col(vw),
        out_shape=jax.ShapeDtypeStruct((t, MLA_HEADS * V_HEAD), BF16),
        compiler_params=_params(1),
        name="attention_meta",
    )(q, k, vt)


def _mixer_out_ffn_kernel(h_ref, o_ref, mconv_ref, gmla_ref, wbrm_ref, wo_ref, g1_ref, b1_ref,
                          wup_ref, wdown_ref, g2_ref, b2_ref, out_ref, act_ref, *, alpha):
    ym = _dot(o_ref[...], wbrm_ref[...])
    merged = mconv_ref[...].astype(F32) + gmla_ref[...].astype(F32) * ym
    mix = _dot(merged.astype(BF16), wo_ref[...])
    h2 = _layer_norm(alpha * h_ref[...] + mix, g1_ref[...], b1_ref[...])
    out_ref[...] = _ffn_ln_block(h2, wup_ref, wdown_ref, g2_ref, b2_ref, act_ref, alpha)


def _mixer_out_ffn(h, o, mconv, gmla, wbrm, wo, g1, b1, wup, wdown, g2, b2, *, alpha, tm):
    n, d = h.shape
    row = lambda width: pl.BlockSpec((tm, width), lambda i: (i, 0))
    weights = [wbrm, wo, g1, b1, wup, wdown, g2, b2]
    return pl.pallas_call(
        functools.partial(_mixer_out_ffn_kernel, alpha=alpha),
        grid=(n // tm,),
        in_specs=[row(d), row(o.shape[1]), row(d), row(d)] + [_resident(a.shape) for a in weights],
        out_specs=row(d),
        out_shape=jax.ShapeDtypeStruct((n, d), F32),
        scratch_shapes=[pltpu.VMEM((tm, wdown.shape[0]), BF16)],
        compiler_params=_params(1),
        name="mixer_out_ffn",
    )(h, o, mconv, gmla, *weights)


def _pack_mixer(w_in, b_gate, conv_w, q_norm_g, w_uq, kv_norm_g, w_ukv, w_br_conv):
    d = w_in.shape[0]
    half_rope = QK_ROPE // 2
    c_kr = 3 * D_CONV + Q_LORA + KV_LORA
    w_in = w_in.astype(BF16)
    kr = w_in[:, c_kr:c_kr + QK_ROPE]
    kr_swapped = jnp.concatenate([kr[:, half_rope:], kr[:, :half_rope]], axis=1)
    kr_block = jnp.concatenate([kr, kr_swapped, jnp.zeros((d, LANES - 2 * QK_ROPE), BF16)], axis=1)
    win = jnp.concatenate([w_in[:, :c_kr], kr_block, w_in[:, c_kr + QK_ROPE:]], axis=1)
    assert win.shape[1] == D_IN_EXT

    wq = w_uq.astype(BF16).reshape(Q_LORA, MLA_HEADS, QK_NOPE + QK_ROPE)
    nope, rope = wq[:, :, :QK_NOPE], wq[:, :, QK_NOPE:]
    rope_swapped = jnp.concatenate([rope[:, :, half_rope:], rope[:, :, :half_rope]], axis=2)
    pad = jnp.zeros((Q_LORA, MLA_HEADS, LANES - QK_NOPE - QK_ROPE), BF16)
    qa = jnp.concatenate([nope, rope, pad], axis=2).reshape(Q_LORA, MLA_HEADS * LANES)
    qb = jnp.concatenate([jnp.zeros_like(nope), rope_swapped, pad], axis=2).reshape(Q_LORA, MLA_HEADS * LANES)

    wkv = w_ukv.astype(BF16).reshape(KV_LORA, MLA_HEADS, QK_NOPE + V_HEAD)
    k_nope = jnp.concatenate([wkv[:, :, :QK_NOPE],
                              jnp.zeros((KV_LORA, MLA_HEADS, LANES - QK_NOPE), BF16)], axis=2)
    place = jnp.concatenate([jnp.zeros((QK_ROPE, QK_NOPE), BF16), jnp.eye(QK_ROPE, dtype=BF16),
                             jnp.zeros((QK_ROPE, LANES - QK_NOPE - QK_ROPE), BF16)], axis=1)
    wk = jnp.concatenate([k_nope.reshape(KV_LORA, MLA_HEADS * LANES),
                          jnp.tile(place, (1, MLA_HEADS)),
                          jnp.zeros((LANES - QK_ROPE, MLA_HEADS * LANES), BF16)], axis=0)
    wv = wkv[:, :, QK_NOPE:].reshape(KV_LORA, MLA_HEADS * V_HEAD)
    return dict(win=win, convw=conv_w, bg=b_gate, qg=q_norm_g[None, :],
                wq=jnp.concatenate([qa, qb], axis=1), kvg=kv_norm_g[None, :],
                wk=wk, wvt=wv.T, wbrc=w_br_conv.astype(BF16))


def _rope_table(positions):
    inv_freq = 1.0 / (ROPE_BASE ** (jnp.arange(0, QK_ROPE, 2, dtype=F32) / QK_ROPE))
    ang = positions.astype(F32)[:, None] * inv_freq[None, :]
    cos, sin = jnp.cos(ang), jnp.sin(ang)
    t = positions.shape[0]
    c = (QK_NOPE + QK_ROPE) ** -0.5 * math.log2(math.e)
    zeros = lambda w: jnp.zeros((t, w), F32)
    pad = LANES - QK_NOPE - QK_ROPE
    q_cos = jnp.concatenate([jnp.full((t, QK_NOPE), c, F32), c * cos, c * cos, zeros(pad)], axis=1)
    q_sin = jnp.concatenate([zeros(QK_NOPE), -c * sin, c * sin, zeros(pad)], axis=1)
    k_cos = jnp.concatenate([cos, cos, zeros(LANES - QK_ROPE)], axis=1)
    k_sin = jnp.concatenate([-sin, sin, zeros(LANES - QK_ROPE)], axis=1)
    return jnp.concatenate([q_cos, q_sin, k_cos, k_sin], axis=1)


def kernel(x, meta_tokens, ffn1_w_up, ffn1_w_down, mix_w_in, mix_b_gate, conv_w, q_norm_g, w_uq,
           kv_norm_g, w_ukv, w_br_conv, w_br_mla, w_o, ffn2_w_up, ffn2_w_down, ln_g, ln_b):
    batch, seq, d = x.shape
    depth = ffn1_w_up.shape[0]
    alpha = (2 * depth) ** 0.25
    n_meta = meta_tokens.shape[0]
    assert n_meta == N_META
    tm = min(ROW_TILE, seq)
    tq = min(ATTN_QUERIES, seq)
    assert seq % tm == 0 and seq % tq == 0

    tab_x = _rope_table(jnp.arange(n_meta, n_meta + seq))
    tab_m = _rope_table(jnp.arange(n_meta))
    zero_halo = jnp.zeros((SUBLANES, D_CONV), F32)

    hx = x.reshape(batch * seq, d)
    hm = meta_tokens.astype(x.dtype)
    for l in range(depth):
        ffn1 = (ffn1_w_up[l].astype(BF16), ffn1_w_down[l].astype(BF16))
        ffn2 = (ffn2_w_up[l].astype(BF16), ffn2_w_down[l].astype(BF16))
        mixw = _pack_mixer(mix_w_in[l], mix_b_gate[l], conv_w[l], q_norm_g[l], w_uq[l],
                           kv_norm_g[l], w_ukv[l], w_br_conv[l])
        wbrm = w_br_mla[l].astype(BF16)
        wo = w_o[l].astype(BF16)
        g = [ln_g[l, j][None, :] for j in range(3)]
        b = [ln_b[l, j][None, :] for j in range(3)]

        hm = _ffn_ln(hm, *ffn1, g[0], b[0], alpha=alpha, tm=n_meta)
        hx = _ffn_ln(hx, *ffn1, g[0], b[0], alpha=alpha, tm=tm)

        mconv_m, gmla_m, q_m, k_m, vt_m, utail_m = _mixer_in(
            hm, tab_m, zero_halo, mixw, tm=n_meta, tiles_per_seq=1)
        mconv_x, gmla_x, q_x, k_x, vt_x, _ = _mixer_in(
            hx, tab_x, utail_m, mixw, tm=tm, tiles_per_seq=seq // tm)

        o_x = _attention(q_x, k_x, vt_x, k_m, vt_m, batch=batch, seq=seq, tq=tq)
        hx = _mixer_out_ffn(hx, o_x, mconv_x, gmla_x, wbrm, wo, g[1], b[1], *ffn2, g[2], b[2],
                            alpha=alpha, tm=tm)

        if l + 1 < depth:
            o_m = _attention_meta(q_m, k_m, vt_m)
            hm = _mixer_out_ffn(hm, o_m, mconv_m, gmla_m, wbrm, wo, g[1], b[1], *ffn2, g[2], b[2],
                                alpha=alpha, tm=n_meta)
    return hx.reshape(batch, seq, d)
```
